```python
import math
import jax, jax.numpy as jnp
from jax import lax
import numpy as np

D_MODEL = 1024
BATCH = 8
SEQ = 8192
DEPTH = 1

CHUNK = 64
QBLOCK = 128
SPATIAL_CHUNK = 128
MIX_WIDTH = D_MODEL
MIX_GROUP_W = MIX_WIDTH // 2
SB_HEADS = 8
SB_HEAD_DIM = MIX_GROUP_W // SB_HEADS
SGU_GROUPS = 8
SGU_GROUP_DIM = MIX_GROUP_W // SGU_GROUPS
IN_PROJ_W = 5 * MIX_GROUP_W
D_FF = int(math.ceil((8 * D_MODEL / 3) / 256) * 256)
EPS = 1e-6

kernel_name = "hybrid_stickbreak_sgu_block"


def rms_norm(x, g):
    xf = x.astype(jnp.float32)
    y = xf * lax.rsqrt(jnp.mean(xf * xf, axis=-1, keepdims=True) + EPS)
    return (y * g.astype(jnp.float32)).astype(x.dtype)


def stick_breaking_attention(q, k, v):
    S = q.shape[1]
    dh = q.shape[-1]
    scale = dh ** -0.5
    n_blocks = S // QBLOCK
    outs = []
    for i in range(n_blocks):
        kv_len = (i + 1) * QBLOCK
        q_blk = q[:, i * QBLOCK:(i + 1) * QBLOCK]
        k_pre = k[:, :kv_len]
        v_pre = v[:, :kv_len]
        z = jnp.einsum('bqhd,bkhd->bhqk', q_blk, k_pre).astype(jnp.float32) * scale
        t_pos = i * QBLOCK + jnp.arange(QBLOCK)
        s_pos = jnp.arange(kv_len)
        strict = s_pos[None, :] < t_pos[:, None]
        log_fail = jnp.where(strict, jax.nn.log_sigmoid(-z), 0.0)
        after = lax.cumsum(log_fail, axis=3, reverse=True) - log_fail
        log_a = jax.nn.log_sigmoid(z) + after
        a = jnp.where(strict, jnp.exp(log_a), 0.0)
        outs.append(jnp.einsum('bhqk,bkhd->bqhd', a.astype(v.dtype), v_pre))
    return jnp.concatenate(outs, axis=1)


def chunk_causal_spatial_mask():
    pos = jnp.arange(SPATIAL_CHUNK)
    return (pos[None, :] // CHUNK) <= (pos[:, None] // CHUNK)


def spatial_gating(u, g, w_spatial, b_spatial):
    B, S, G, Dg = g.shape
    n = S // SPATIAL_CHUNK
    g_c = g.reshape(B, n, SPATIAL_CHUNK, G, Dg)
    w = jnp.where(chunk_causal_spatial_mask()[None], w_spatial, 0.0).astype(g.dtype)
    s = jnp.einsum('gij,bnjgd->bnigd', w, g_c) + b_spatial.T[None, None, :, :, None].astype(g.dtype)
    return u * s.reshape(B, S, G, Dg)


def setup_inputs(seed: int = 0) -> dict:
    key = jax.random.key(seed)
    ks = jax.random.split(key, 16)
    f32 = jnp.float32

    def nrm(k, shape, scale):
        return jax.random.normal(k, shape, f32) * scale

    return {
        "x": nrm(ks[0], (BATCH, SEQ, D_MODEL), 1.0),
        "attn_norm_g": 1.0 + nrm(ks[1], (D_MODEL,), 0.02),
        "w_in": nrm(ks[2], (D_MODEL, IN_PROJ_W), D_MODEL ** -0.5),
        "q_norm_g": 1.0 + nrm(ks[3], (SB_HEAD_DIM,), 0.02),
        "k_norm_g": 1.0 + nrm(ks[4], (SB_HEAD_DIM,), 0.02),
        "sgu_norm_g": 1.0 + nrm(ks[5], (MIX_GROUP_W,), 0.02),
        "w_spatial": nrm(ks[6], (SGU_GROUPS, SPATIAL_CHUNK, SPATIAL_CHUNK), SPATIAL_CHUNK ** -0.5),
        "b_spatial": nrm(ks[7], (SGU_GROUPS, SPATIAL_CHUNK), 0.02),
        "sb_out_norm_g": 1.0 + nrm(ks[8], (MIX_GROUP_W,), 0.02),
        "sgu_out_norm_g": 1.0 + nrm(ks[9], (MIX_GROUP_W,), 0.02),
        "w_out": nrm(ks[10], (MIX_WIDTH, D_MODEL), MIX_WIDTH ** -0.5),
        "ffn_norm_g": 1.0 + nrm(ks[11], (D_MODEL,), 0.02),
        "w_gate": nrm(ks[12], (D_MODEL, D_FF), D_MODEL ** -0.5),
        "w_up": nrm(ks[13], (D_MODEL, D_FF), D_MODEL ** -0.5),
        "w_down": nrm(ks[14], (D_FF, D_MODEL), D_FF ** -0.5),
    }


def reference(x, attn_norm_g, w_in, q_norm_g, k_norm_g, sgu_norm_g, w_spatial, b_spatial,
              sb_out_norm_g, sgu_out_norm_g, w_out, ffn_norm_g, w_gate, w_up, w_down):
    B, S, _ = x.shape
    for _layer in range(DEPTH):
        h = rms_norm(x, attn_norm_g)
        proj = jnp.einsum('bsd,de->bse', h, w_in)
        q, k, v, u, g = jnp.split(proj, 5, axis=-1)

        q = rms_norm(q.reshape(B, S, SB_HEADS, SB_HEAD_DIM), q_norm_g)
        k = rms_norm(k.reshape(B, S, SB_HEADS, SB_HEAD_DIM), k_norm_g)
        v = v.reshape(B, S, SB_HEADS, SB_HEAD_DIM)
        o_sb = stick_breaking_attention(q, k, v).reshape(B, S, MIX_GROUP_W)

        u = jax.nn.gelu(u)
        g = rms_norm(jax.nn.gelu(g), sgu_norm_g)
        o_sgu = spatial_gating(u.reshape(B, S, SGU_GROUPS, SGU_GROUP_DIM),
                               g.reshape(B, S, SGU_GROUPS, SGU_GROUP_DIM),
                               w_spatial, b_spatial).reshape(B, S, MIX_GROUP_W)

        mixed = jnp.concatenate([rms_norm(o_sb, sb_out_norm_g), rms_norm(o_sgu, sgu_out_norm_g)], axis=-1)
        x = x + jnp.einsum('bse,ed->bsd', mixed, w_out)

        h2 = rms_norm(x, ffn_norm_g)
        ff = jax.nn.silu(jnp.einsum('bsd,df->bsf', h2, w_gate)) * jnp.einsum('bsd,df->bsf', h2, w_up)
        x = x + jnp.einsum('bsf,fd->bsd', ff, w_down)
    return x
```

```python
import functools
import math

import jax
import jax.numpy as jnp
from jax import lax
from jax.experimental import pallas as pl
from jax.experimental.pallas import tpu as pltpu

EPS = 1e-6
HEAD_DIM = 64
SPATIAL_CHUNK = 128
STREAM_CHUNK = 64
LANES = 128
HEADS_PER_GROUP = LANES // HEAD_DIM
VMEM_LIMIT_BYTES = 56 * 1024 * 1024

F32 = jnp.float32
BF16 = jnp.bfloat16


def _rms_scale(v, width):
    return lax.rsqrt(jnp.sum(v * v, axis=-1, keepdims=True) * (1.0 / width) + EPS)


def _dot(a, b):
    return jnp.dot(a, b, preferred_element_type=F32)


def _in_proj_kernel(x_ref, ng_ref, w_ref, qg_ref, kg_ref, sg_ref, gsum_ref,
                    q_ref, k_ref, v_ref, u_ref, g_ref):
    gw = q_ref.shape[-1]
    x = x_ref[...]
    h = (x * _rms_scale(x, x.shape[-1]) * ng_ref[...]).astype(BF16)

    def head_norm(p, gain):
        ssq = _dot((p * p).astype(BF16), gsum_ref[...])
        return p * lax.rsqrt(ssq * (1.0 / HEAD_DIM) + EPS) * gain

    q_ref[...] = head_norm(_dot(h, w_ref[:, 0 * gw:1 * gw]), qg_ref[...]).astype(BF16)
    k_ref[...] = head_norm(_dot(h, w_ref[:, 1 * gw:2 * gw]), kg_ref[...]).astype(BF16)
    v_ref[...] = _dot(h, w_ref[:, 2 * gw:3 * gw]).astype(BF16)
    u_ref[...] = jax.nn.gelu(_dot(h, w_ref[:, 3 * gw:4 * gw])).astype(BF16)
    gg = jax.nn.gelu(_dot(h, w_ref[:, 4 * gw:5 * gw]))
    g_ref[...] = (gg * _rms_scale(gg, gw) * sg_ref[...]).astype(BF16)


def _in_proj(x2d, attn_norm_g, w_in, q_gain, k_gain, sgu_norm_g, gsum, *, tm):
    n, d = x2d.shape
    gw = w_in.shape[1] // 5
    row = lambda i: (i, 0)
    const = lambda i: (0, 0)
    out = jax.ShapeDtypeStruct((n, gw), BF16)
    return pl.pallas_call(
        _in_proj_kernel,
        grid=(n // tm,),
        in_specs=[
            pl.BlockSpec((tm, d), row),
            pl.BlockSpec((1, d), const),
            pl.BlockSpec((d, 5 * gw), const),
            pl.BlockSpec((1, gw), const),
            pl.BlockSpec((1, gw), const),
            pl.BlockSpec((1, gw), const),
            pl.BlockSpec((gw, gw), const),
        ],
        out_specs=[pl.BlockSpec((tm, gw), row)] * 5,
        out_shape=[out] * 5,
        compiler_params=pltpu.CompilerParams(
            dimension_semantics=("parallel",), vmem_limit_bytes=VMEM_LIMIT_BYTES),
        name="in_proj",
    )(x2d, attn_norm_g, w_in, q_gain, k_gain, sgu_norm_g, gsum)


def _attn_kernel(q_ref, k_ref, v_ref, tri_ref, o_ref, acc_ref, *, tq):
    i = pl.program_id(2)
    kb = tq
    m = HEADS_PER_GROUP * tq
    q = q_ref[0]
    lane_head = lax.broadcasted_iota(jnp.int32, q.shape, 1) // HEAD_DIM
    qs = jnp.concatenate(
        [jnp.where(lane_head == p, q, jnp.zeros_like(q)) for p in range(HEADS_PER_GROUP)], axis=0)
    tri = tri_ref[...]

    def block(j, carry, strict):
        start = pl.multiple_of(j * kb, kb)
        kblk = k_ref[0, pl.ds(start, kb), :]
        vblk = v_ref[0, pl.ds(start, kb), :]
        z = lax.dot_general(qs, kblk, (((1,), (1,)), ((), ())), preferred_element_type=F32)
        sp = jnp.maximum(z, 0.0) + jnp.log(1.0 + jnp.exp(-jnp.abs(z)))
        ls = z - sp
        if strict is not None:
            sp = jnp.where(strict, sp, 0.0)
        hi = sp.astype(BF16)
        lo = (sp - hi.astype(F32)).astype(BF16)
        after = _dot(hi, tri) + _dot(lo, tri) + carry
        a = jnp.exp(ls + after)
        if strict is not None:
            a = jnp.where(strict, a, 0.0)
        acc_ref[...] += _dot(a.astype(BF16), vblk)
        return carry - jnp.sum(sp, axis=1, keepdims=True)

    acc_ref[...] = jnp.zeros_like(acc_ref)
    rows = lax.broadcasted_iota(jnp.int32, (m, kb), 0) % tq
    cols = lax.broadcasted_iota(jnp.int32, (m, kb), 1)
    carry = block(i, jnp.zeros((m, 1), F32), cols < rows)
    lax.fori_loop(0, i, lambda n, c: block(i - 1 - n, c, None), carry)

    out = acc_ref[0:tq, :]
    for p in range(1, HEADS_PER_GROUP):
        out = jnp.where(lane_head == p, acc_ref[p * tq:(p + 1) * tq, :], out)
    o_ref[0] = out.astype(BF16)


def _attention(q, k, v, tri, *, tq):
    b, s, gw = q.shape
    kernel = functools.partial(_attn_kernel, tq=tq)
    return pl.pallas_call(
        kernel,
        grid=(b, gw // LANES, s // tq),
        in_specs=[
            pl.BlockSpec((1, tq, LANES), lambda bi, hi, qi: (bi, qi, hi)),
            pl.BlockSpec((1, s, LANES), lambda bi, hi, qi: (bi, 0, hi)),
            pl.BlockSpec((1, s, LANES), lambda bi, hi, qi: (bi, 0, hi)),
            pl.BlockSpec((tq, tq), lambda bi, hi, qi: (0, 0)),
        ],
        out_specs=pl.BlockSpec((1, tq, LANES), lambda bi, hi, qi: (bi, qi, hi)),
        out_shape=jax.ShapeDtypeStruct((b, s, gw), BF16),
        scratch_shapes=[pltpu.VMEM((HEADS_PER_GROUP * tq, LANES), F32)],
        compiler_params=pltpu.CompilerParams(
            dimension_semantics=("parallel", "parallel", "arbitrary"),
            vmem_limit_bytes=VMEM_LIMIT_BYTES),
        name="sb_attn",
    )(q, k, v, tri)


def _mix_kernel(x_ref, osb_ref, u_ref, g_ref, wsp_ref, bias_ref, sbg_ref, sgg_ref, wout_ref, o_ref,
                osgu_ref):
    tm, gw = u_ref.shape
    n_groups = wsp_ref.shape[0]
    ri = lax.broadcasted_iota(jnp.int32, (SPATIAL_CHUNK, SPATIAL_CHUNK), 0) // STREAM_CHUNK
    ci = lax.broadcasted_iota(jnp.int32, (SPATIAL_CHUNK, SPATIAL_CHUNK), 1) // STREAM_CHUNK
    w = [jnp.where(ci <= ri, wsp_ref[gi], 0.0).astype(BF16) for gi in range(n_groups)]
    low_half = lax.broadcasted_iota(jnp.int32, (SPATIAL_CHUNK, LANES), 1) < HEAD_DIM
    for c in range(tm // SPATIAL_CHUNK):
        r0 = c * SPATIAL_CHUNK
        for p in range(gw // LANES):
            l0 = p * LANES
            gp = g_ref[r0:r0 + SPATIAL_CHUNK, l0:l0 + LANES]
            s = jnp.where(low_half, _dot(w[2 * p], gp), _dot(w[2 * p + 1], gp))
            s = s + bias_ref[:, l0:l0 + LANES]
            osgu_ref[r0:r0 + SPATIAL_CHUNK, l0:l0 + LANES] = (
                u_ref[r0:r0 + SPATIAL_CHUNK, l0:l0 + LANES].astype(F32) * s)
    osgu = osgu_ref[...]
    sgn = (osgu * _rms_scale(osgu, gw) * sgg_ref[...]).astype(BF16)
    osb = osb_ref[...].astype(F32)
    sbn = (osb * _rms_scale(osb, gw) * sbg_ref[...]).astype(BF16)
    o_ref[...] = x_ref[...] + _dot(sbn, wout_ref[0:gw, :]) + _dot(sgn, wout_ref[gw:2 * gw, :])


def _mix(x2d, osb, u, g, w_spatial, bias, sb_gain, sgu_gain, w_out, *, tm):
    n, d = x2d.shape
    gw = u.shape[1]
    row = lambda i: (i, 0)
    const = lambda i: (0, 0)
    return pl.pallas_call(
        _mix_kernel,
        grid=(n // tm,),
        in_specs=[
            pl.BlockSpec((tm, d), row),
            pl.BlockSpec((tm, gw), row),
            pl.BlockSpec((tm, gw), row),
            pl.BlockSpec((tm, gw), row),
            pl.BlockSpec(w_spatial.shape, lambda i: (0, 0, 0)),
            pl.BlockSpec(bias.shape, const),
            pl.BlockSpec((1, gw), const),
            pl.BlockSpec((1, gw), const),
            pl.BlockSpec(w_out.shape, const),
        ],
        out_specs=pl.BlockSpec((tm, d), row),
        out_shape=jax.ShapeDtypeStruct((n, d), F32),
        scratch_shapes=[pltpu.VMEM((tm, gw), F32)],
        compiler_params=pltpu.CompilerParams(
            dimension_semantics=("parallel",), vmem_limit_bytes=VMEM_LIMIT_BYTES),
        name="mix",
    )(x2d, osb, u, g, w_spatial, bias, sb_gain, sgu_gain, w_out)


def _ffn_kernel(x_ref, ng_ref, wg_ref, wu_ref, wd_ref, o_ref):
    x = x_ref[...]
    h = (x * _rms_scale(x, x.shape[-1]) * ng_ref[...]).astype(BF16)
    gate = _dot(h, wg_ref[...])
    up = _dot(h, wu_ref[...])
    ff = (gate * jax.nn.sigmoid(gate) * up).astype(BF16)
    o_ref[...] = x + _dot(ff, wd_ref[...])


def _ffn(x2d, ffn_norm_g, w_gate, w_up, w_down, *, tm):
    n, d = x2d.shape
    row = lambda i: (i, 0)
    const = lambda i: (0, 0)
    resident = functools.partial(pl.BlockSpec, index_map=const, pipeline_mode=pl.Buffered(1))
    return pl.pallas_call(
        _ffn_kernel,
        grid=(n // tm,),
        in_specs=[
            pl.BlockSpec((tm, d), row),
            pl.BlockSpec((1, d), const),
            resident(w_gate.shape),
            resident(w_up.shape),
            resident(w_down.shape),
        ],
        out_specs=pl.BlockSpec((tm, d), row),
        out_shape=jax.ShapeDtypeStruct((n, d), F32),
        compiler_params=pltpu.CompilerParams(
            dimension_semantics=("parallel",), vmem_limit_bytes=VMEM_LIMIT_BYTES),
        name="ffn",
    )(x2d, ffn_norm_g, w_gate, w_up, w_down)


def _token_tile(s, preferred=512):
    tm = min(preferred, s)
    assert s % tm == 0 and tm % SPATIAL_CHUNK == 0
    return tm


def _query_tile(s, preferred=256):
    tq = min(preferred, s)
    assert s % tq == 0 and tq % LANES == 0
    return tq


def kernel(x, attn_norm_g, w_in, q_norm_g, k_norm_g, sgu_norm_g, w_spatial, b_spatial,
           sb_out_norm_g, sgu_out_norm_g, w_out, ffn_norm_g, w_gate, w_up, w_down):
    b, s, d = x.shape
    gw = w_in.shape[1] // 5
    n_heads = gw // HEAD_DIM
    assert gw % LANES == 0 and w_spatial.shape[1] == SPATIAL_CHUNK
    tm = _token_tile(s)
    tq = _query_tile(s)
    row = lambda a: a.reshape(1, -1).astype(F32)

    q_gain = row(jnp.tile(q_norm_g, n_heads) * (HEAD_DIM ** -0.5))
    k_gain = row(jnp.tile(k_norm_g, n_heads))
    head_of = jnp.arange(gw) // HEAD_DIM
    gsum = (head_of[:, None] == head_of[None, :]).astype(BF16)
    pos = jnp.arange(tq)
    tri = -(pos[:, None] > pos[None, :]).astype(BF16)
    bias = jnp.repeat(b_spatial.T.astype(F32), HEAD_DIM, axis=1)

    x2d = x.reshape(b * s, d)
    q, k, v, u, g = _in_proj(x2d, row(attn_norm_g), w_in.astype(BF16), q_gain, k_gain,
                             row(sgu_norm_g), gsum, tm=tm)
    osb = _attention(q.reshape(b, s, gw), k.reshape(b, s, gw), v.reshape(b, s, gw), tri, tq=tq)
    x1 = _mix(x2d, osb.reshape(b * s, gw), u, g, w_spatial, bias, row(sb_out_norm_g),
              row(sgu_out_norm_g), w_out.astype(BF16), tm=tm)
    y = _ffn(x1, row(ffn_norm_g), w_gate.astype(BF16), w_up.astype(BF16), w_down.astype(BF16), tm=tm)
    return y.reshape(b, s, d)
```

```python
import functools
import math

import jax
import jax.numpy as jnp
from jax import lax
from jax.experimental import pallas as pl
from jax.experimental.pallas import tpu as pltpu

EPS = 1e-6
HEAD_DIM = 64
SPATIAL_CHUNK = 128
STREAM_CHUNK = 64
LANES = 128
HEADS_PER_GROUP = LANES // HEAD_DIM
VMEM_LIMIT_BYTES = 56 * 1024 * 1024
LOG2_WEIGHT_UNDERFLOW = -150.0

F32 = jnp.float32
BF16 = jnp.bfloat16


def _rms_scale(v, width):
    return lax.rsqrt(jnp.sum(v * v, axis=-1, keepdims=True) * (1.0 / width) + EPS)


def _dot(a, b):
    return jnp.dot(a, b, preferred_element_type=F32)


def _in_proj_kernel(x_ref, ng_ref, w_ref, qg_ref, kg_ref, sg_ref, gsum_ref,
                    q_ref, k_ref, v_ref, u_ref, g_ref):
    gw = q_ref.shape[-1]
    x = x_ref[...]
    h = (x * _rms_scale(x, x.shape[-1]) * ng_ref[...]).astype(BF16)

    def head_norm(p, gain):
        ssq = _dot((p * p).astype(BF16), gsum_ref[...])
        return p * lax.rsqrt(ssq * (1.0 / HEAD_DIM) + EPS) * gain

    q_ref[...] = head_norm(_dot(h, w_ref[:, 0 * gw:1 * gw]), qg_ref[...]).astype(BF16)
    k_ref[...] = head_norm(_dot(h, w_ref[:, 1 * gw:2 * gw]), kg_ref[...]).astype(BF16)
    v_ref[...] = _dot(h, w_ref[:, 2 * gw:3 * gw]).astype(BF16)
    u_ref[...] = jax.nn.gelu(_dot(h, w_ref[:, 3 * gw:4 * gw])).astype(BF16)
    gg = jax.nn.gelu(_dot(h, w_ref[:, 4 * gw:5 * gw]))
    g_ref[...] = (gg * _rms_scale(gg, gw) * sg_ref[...]).astype(BF16)


def _in_proj(x2d, attn_norm_g, w_in, q_gain, k_gain, sgu_norm_g, gsum, *, tm):
    n, d = x2d.shape
    gw = w_in.shape[1] // 5
    row = lambda i: (i, 0)
    const = lambda i: (0, 0)
    out = jax.ShapeDtypeStruct((n, gw), BF16)
    return pl.pallas_call(
        _in_proj_kernel,
        grid=(n // tm,),
        in_specs=[
            pl.BlockSpec((tm, d), row),
            pl.BlockSpec((1, d), const),
            pl.BlockSpec((d, 5 * gw), const),
            pl.BlockSpec((1, gw), const),
            pl.BlockSpec((1, gw), const),
            pl.BlockSpec((1, gw), const),
            pl.BlockSpec((gw, gw), const),
        ],
        out_specs=[pl.BlockSpec((tm, gw), row)] * 5,
        out_shape=[out] * 5,
        compiler_params=pltpu.CompilerParams(
            dimension_semantics=("parallel",), vmem_limit_bytes=VMEM_LIMIT_BYTES),
        name="in_proj",
    )(x2d, attn_norm_g, w_in, q_gain, k_gain, sgu_norm_g, gsum)


def _attn_kernel(q_ref, k_ref, v_ref, tri_ref, o_ref, acc_ref, *, tq):
    i = pl.program_id(2)
    kb = tq
    m = HEADS_PER_GROUP * tq
    q = q_ref[0]
    lane_head = lax.broadcasted_iota(jnp.int32, q.shape, 1) // HEAD_DIM
    qs = jnp.concatenate(
        [jnp.where(lane_head == p, q, jnp.zeros_like(q)) for p in range(HEADS_PER_GROUP)], axis=0)
    tri = tri_ref[...]

    def scores(j):
        kblk = k_ref[0, pl.ds(pl.multiple_of(j * kb, kb), kb), :]
        z = lax.dot_general(qs, kblk, (((1,), (1,)), ((), ())), preferred_element_type=F32)
        neg_abs = lax.bitcast_convert_type(
            lax.bitcast_convert_type(z, jnp.uint32) | jnp.uint32(0x80000000), F32)
        return z, jnp.maximum(z, 0.0) + jnp.log2(1.0 + jnp.exp2(neg_abs))

    def local_cumsum(sp):
        hi = sp.astype(BF16)
        lo = (sp - hi.astype(F32)).astype(BF16)
        return _dot(jnp.concatenate([hi, lo], axis=1), tri)

    def weights(z, sp, ab, carry):
        return jnp.exp2((z - sp) + (ab + carry))

    def weighted_values(a, j):
        return _dot(a.astype(BF16), v_ref[0, pl.ds(pl.multiple_of(j * kb, kb), kb), :])

    def advance(carry, sp, ab):
        return carry + (ab[:, 0:1] - sp[:, 0:1])

    rows = lax.broadcasted_iota(jnp.int32, (m, kb), 0) % tq
    cols = lax.broadcasted_iota(jnp.int32, (m, kb), 1)
    strict = cols < rows
    has_prev = i > 0
    jp = jnp.maximum(i - 1, 0)
    z1, sp1 = scores(i)
    z2, sp2 = scores(jp)
    sp1 = jnp.where(strict, sp1, 0.0)
    sp2 = jnp.where(has_prev, sp2, 0.0)
    ab1 = local_cumsum(sp1)
    ab2 = local_cumsum(sp2)
    carry = advance(jnp.zeros((m, 1), F32), sp1, ab1)
    a1 = jnp.where(strict, weights(z1, sp1, ab1, 0.0), 0.0)
    a2 = jnp.where(has_prev, weights(z2, sp2, ab2, carry), 0.0)
    acc_ref[...] = weighted_values(a1, i) + weighted_values(a2, jp)
    carry = advance(carry, sp2, ab2)

    def more(state):
        j, _, live = state
        return jnp.logical_and(j >= 0, live)

    def step(state):
        j, carry, _ = state
        z, sp = scores(j)
        ab = local_cumsum(sp)
        acc_ref[...] += weighted_values(weights(z, sp, ab, carry), j)
        carry = advance(carry, sp, ab)
        return j - 1, carry, jnp.max(carry) > LOG2_WEIGHT_UNDERFLOW

    lax.while_loop(more, step, (i - 2, carry, jnp.max(carry) > LOG2_WEIGHT_UNDERFLOW))

    out = acc_ref[0:tq, :]
    for p in range(1, HEADS_PER_GROUP):
        out = jnp.where(lane_head == p, acc_ref[p * tq:(p + 1) * tq, :], out)
    o_ref[0] = out.astype(BF16)


def _attention(q, k, v, tri, *, tq):
    b, s, gw = q.shape
    kernel = functools.partial(_attn_kernel, tq=tq)
    return pl.pallas_call(
        kernel,
        grid=(b, gw // LANES, s // tq),
        in_specs=[
            pl.BlockSpec((1, tq, LANES), lambda bi, hi, qi: (bi, qi, hi)),
            pl.BlockSpec((1, s, LANES), lambda bi, hi, qi: (bi, 0, hi)),
            pl.BlockSpec((1, s, LANES), lambda bi, hi, qi: (bi, 0, hi)),
            pl.BlockSpec((2 * tq, tq), lambda bi, hi, qi: (0, 0)),
        ],
        out_specs=pl.BlockSpec((1, tq, LANES), lambda bi, hi, qi: (bi, qi, hi)),
        out_shape=jax.ShapeDtypeStruct((b, s, gw), BF16),
        scratch_shapes=[pltpu.VMEM((HEADS_PER_GROUP * tq, LANES), F32)],
        compiler_params=pltpu.CompilerParams(
            dimension_semantics=("parallel", "parallel", "arbitrary"),
            vmem_limit_bytes=VMEM_LIMIT_BYTES),
        name="sb_attn",
    )(q, k, v, tri)


def _mix_kernel(x_ref, osb_ref, u_ref, g_ref, wsp_ref, bias_ref, sbg_ref, sgg_ref, wout_ref, o_ref,
                osgu_ref):
    tm, gw = u_ref.shape
    n_groups = wsp_ref.shape[0]
    ri = lax.broadcasted_iota(jnp.int32, (SPATIAL_CHUNK, SPATIAL_CHUNK), 0) // STREAM_CHUNK
    ci = lax.broadcasted_iota(jnp.int32, (SPATIAL_CHUNK, SPATIAL_CHUNK), 1) // STREAM_CHUNK
    w = [jnp.where(ci <= ri, wsp_ref[gi], 0.0).astype(BF16) for gi in range(n_groups)]
    low_half = lax.broadcasted_iota(jnp.int32, (SPATIAL_CHUNK, LANES), 1) < HEAD_DIM
    for c in range(tm // SPATIAL_CHUNK):
        r0 = c * SPATIAL_CHUNK
        for p in range(gw // LANES):
            l0 = p * LANES
            gp = g_ref[r0:r0 + SPATIAL_CHUNK, l0:l0 + LANES]
            s = jnp.where(low_half, _dot(w[2 * p], gp), _dot(w[2 * p + 1], gp))
            s = s + bias_ref[:, l0:l0 + LANES]
            osgu_ref[r0:r0 + SPATIAL_CHUNK, l0:l0 + LANES] = (
                u_ref[r0:r0 + SPATIAL_CHUNK, l0:l0 + LANES].astype(F32) * s)
    osgu = osgu_ref[...]
    sgn = (osgu * _rms_scale(osgu, gw) * sgg_ref[...]).astype(BF16)
    osb = osb_ref[...].astype(F32)
    sbn = (osb * _rms_scale(osb, gw) * sbg_ref[...]).astype(BF16)
    o_ref[...] = x_ref[...] + _dot(sbn, wout_ref[0:gw, :]) + _dot(sgn, wout_ref[gw:2 * gw, :])


def _mix(x2d, osb, u, g, w_spatial, bias, sb_gain, sgu_gain, w_out, *, tm):
    n, d = x2d.shape
    gw = u.shape[1]
    row = lambda i: (i, 0)
    const = lambda i: (0, 0)
    return pl.pallas_call(
        _mix_kernel,
        grid=(n // tm,),
        in_specs=[
            pl.BlockSpec((tm, d), row),
            pl.BlockSpec((tm, gw), row),
            pl.BlockSpec((tm, gw), row),
            pl.BlockSpec((tm, gw), row),
            pl.BlockSpec(w_spatial.shape, lambda i: (0, 0, 0)),
            pl.BlockSpec(bias.shape, const),
            pl.BlockSpec((1, gw), const),
            pl.BlockSpec((1, gw), const),
            pl.BlockSpec(w_out.shape, const),
        ],
        out_specs=pl.BlockSpec((tm, d), row),
        out_shape=jax.ShapeDtypeStruct((n, d), F32),
        scratch_shapes=[pltpu.VMEM((tm, gw), F32)],
        compiler_params=pltpu.CompilerParams(
            dimension_semantics=("parallel",), vmem_limit_bytes=VMEM_LIMIT_BYTES),
        name="mix",
    )(x2d, osb, u, g, w_spatial, bias, sb_gain, sgu_gain, w_out)


def _ffn_kernel(x_ref, ng_ref, wg_ref, wu_ref, wd_ref, o_ref):
    x = x_ref[...]
    h = (x * _rms_scale(x, x.shape[-1]) * ng_ref[...]).astype(BF16)
    gate = _dot(h, wg_ref[...])
    up = _dot(h, wu_ref[...])
    ff = (gate * jax.nn.sigmoid(gate) * up).astype(BF16)
    o_ref[...] = x + _dot(ff, wd_ref[...])


def _ffn(x2d, ffn_norm_g, w_gate, w_up, w_down, *, tm):
    n, d = x2d.shape
    row = lambda i: (i, 0)
    const = lambda i: (0, 0)
    resident = functools.partial(pl.BlockSpec, index_map=const, pipeline_mode=pl.Buffered(1))
    return pl.pallas_call(
        _ffn_kernel,
        grid=(n // tm,),
        in_specs=[
            pl.BlockSpec((tm, d), row),
            pl.BlockSpec((1, d), const),
            resident(w_gate.shape),
            resident(w_up.shape),
            resident(w_down.shape),
        ],
        out_specs=pl.BlockSpec((tm, d), row),
        out_shape=jax.ShapeDtypeStruct((n, d), F32),
        compiler_params=pltpu.CompilerParams(
            dimension_semantics=("parallel",), vmem_limit_bytes=VMEM_LIMIT_BYTES),
        name="ffn",
    )(x2d, ffn_norm_g, w_gate, w_up, w_down)


def _token_tile(s, preferred=512):
    tm = min(preferred, s)
    assert s % tm == 0 and tm % SPATIAL_CHUNK == 0
    return tm


def _query_tile(s, preferred=256):
    tq = min(preferred, s)
    assert s % tq == 0 and tq % LANES == 0
    return tq


def kernel(x, attn_norm_g, w_in, q_norm_g, k_norm_g, sgu_norm_g, w_spatial, b_spatial,
           sb_out_norm_g, sgu_out_norm_g, w_out, ffn_norm_g, w_gate, w_up, w_down):
    b, s, d = x.shape
    gw = w_in.shape[1] // 5
    n_heads = gw // HEAD_DIM
    assert gw % LANES == 0 and w_spatial.shape[1] == SPATIAL_CHUNK
    tm = _token_tile(s)
    tq = _query_tile(s)
    row = lambda a: a.reshape(1, -1).astype(F32)

    q_gain = row(jnp.tile(q_norm_g, n_heads) * (HEAD_DIM ** -0.5 * math.log2(math.e)))
    k_gain = row(jnp.tile(k_norm_g, n_heads))
    head_of = jnp.arange(gw) // HEAD_DIM
    gsum = (head_of[:, None] == head_of[None, :]).astype(BF16)
    pos = jnp.arange(tq)
    tri = jnp.tile(-(pos[:, None] > pos[None, :]).astype(BF16), (2, 1))
    bias = jnp.repeat(b_spatial.T.astype(F32), HEAD_DIM, axis=1)

    x2d = x.reshape(b * s, d)
    q, k, v, u, g = _in_proj(x2d, row(attn_norm_g), w_in.astype(BF16), q_gain, k_gain,
                             row(sgu_norm_g), gsum, tm=tm)
    osb = _attention(q.reshape(b, s, gw), k.reshape(b, s, gw), v.reshape(b, s, gw), tri, tq=tq)
    x1 = _mix(x2d, osb.reshape(b * s, gw), u, g, w_spatial, bias, row(sb_out_norm_g),
              row(sgu_out_norm_g), w_out.astype(BF16), tm=tm)
    y = _ffn(x1, row(ffn_norm_g), w_gate.astype(BF16), w_up.astype(BF16), w_down.astype(BF16), tm=tm)
    return y.reshape(b, s, d)
```

```python
import functools
import math

import jax
import jax.numpy as jnp
from jax import lax
from jax.experimental import pallas as pl
from jax.experimental.pallas import tpu as pltpu

EPS = 1e-6
HEAD_DIM = 64
SPATIAL_CHUNK = 128
STREAM_CHUNK = 64
LANES = 128
ATTN_LANES = 2 * LANES
ATTN_BLOCK = LANES
ATTN_SUBBLOCKS = 2
WINDOW_BLOCKS = 3
VMEM_LIMIT_BYTES = 56 * 1024 * 1024
LOG2_WEIGHT_UNDERFLOW = -150.0
NO_BLOCK_LOG2_WEIGHT = -1e30

F32 = jnp.float32
BF16 = jnp.bfloat16


def _rms_scale(v, width):
    return lax.rsqrt(jnp.sum(v * v, axis=-1, keepdims=True) * (1.0 / width) + EPS)


def _dot(a, b):
    return jnp.dot(a, b, preferred_element_type=F32)


def _in_proj_kernel(x_ref, ng_ref, w_ref, qg_ref, kg_ref, sg_ref, gsum_ref,
                    q_ref, k_ref, v_ref, u_ref, g_ref):
    gw = q_ref.shape[-1]
    x = x_ref[...]
    h = (x * _rms_scale(x, x.shape[-1]) * ng_ref[...]).astype(BF16)

    def head_norm(p, gain):
        ssq = _dot((p * p).astype(BF16), gsum_ref[...])
        return p * lax.rsqrt(ssq * (1.0 / HEAD_DIM) + EPS) * gain

    q_ref[...] = head_norm(_dot(h, w_ref[:, 0 * gw:1 * gw]), qg_ref[...]).astype(BF16)
    k_ref[...] = head_norm(_dot(h, w_ref[:, 1 * gw:2 * gw]), kg_ref[...]).astype(BF16)
    v_ref[...] = _dot(h, w_ref[:, 2 * gw:3 * gw]).astype(BF16)
    u_ref[...] = jax.nn.gelu(_dot(h, w_ref[:, 3 * gw:4 * gw])).astype(BF16)
    gg = jax.nn.gelu(_dot(h, w_ref[:, 4 * gw:5 * gw]))
    g_ref[...] = (gg * _rms_scale(gg, gw) * sg_ref[...]).astype(BF16)


def _in_proj(x2d, attn_norm_g, w_in, q_gain, k_gain, sgu_norm_g, gsum, *, tm):
    n, d = x2d.shape
    gw = w_in.shape[1] // 5
    row = lambda i: (i, 0)
    const = lambda i: (0, 0)
    out = jax.ShapeDtypeStruct((n, gw), BF16)
    return pl.pallas_call(
        _in_proj_kernel,
        grid=(n // tm,),
        in_specs=[
            pl.BlockSpec((tm, d), row),
            pl.BlockSpec((1, d), const),
            pl.BlockSpec((d, 5 * gw), const),
            pl.BlockSpec((1, gw), const),
            pl.BlockSpec((1, gw), const),
            pl.BlockSpec((1, gw), const),
            pl.BlockSpec((gw, gw), const),
        ],
        out_specs=[pl.BlockSpec((tm, gw), row)] * 5,
        out_shape=[out] * 5,
        compiler_params=pltpu.CompilerParams(
            dimension_semantics=("parallel",), vmem_limit_bytes=VMEM_LIMIT_BYTES),
        name="in_proj",
    )(x2d, attn_norm_g, w_in, q_gain, k_gain, sgu_norm_g, gsum)


def _attn_kernel(q_ref, k_ref, v_ref, tri_ref, o_ref, acc_ref, *, kb):
    n_sub = q_ref.shape[1] // kb
    n_heads = q_ref.shape[-1] // HEAD_DIM
    m = n_heads * kb
    lane_head = lax.broadcasted_iota(jnp.int32, (kb, q_ref.shape[-1]), 1) // HEAD_DIM
    strict = (lax.broadcasted_iota(jnp.int32, (m, kb), 1)
              < lax.broadcasted_iota(jnp.int32, (m, kb), 0) % kb)
    tri = tri_ref[...]

    def rows_of(ref, j):
        return ref[0, pl.ds(pl.multiple_of(j * kb, kb), kb), :]

    def softplus2(z):
        return jnp.maximum(z, 0.0) + jnp.log2(1.0 + jnp.exp2(-jnp.abs(z)))

    def hi_lo(sp):
        hi = sp.astype(BF16)
        return jnp.concatenate([hi, (sp - hi.astype(F32)).astype(BF16)], axis=1)

    def qk(qs, kmat):
        return lax.dot_general(qs, kmat, (((1,), (1,)), ((), ())), preferred_element_type=F32)

    def windows(blocks, qss):
        chains = range(len(blocks))
        piece = lambda t, b: t[:, b * kb:(b + 1) * kb]
        js = [[i - b for b in range(WINDOW_BLOCKS)] for i in blocks]
        jc = [[jnp.maximum(j, 0) for j in row] for row in js]
        z = [qk(qss[c], jnp.concatenate([rows_of(k_ref, j) for j in jc[c]], axis=0)) for c in chains]
        sp = [softplus2(z[c]) for c in chains]
        spb = [[jnp.where(strict, piece(sp[c], 0), 0.0)]
               + [piece(sp[c], b) for b in range(1, WINDOW_BLOCKS)] for c in chains]
        cs = _dot(jnp.concatenate([hi_lo(s) for c in chains for s in spb[c]], axis=0), tri)
        carries = []
        for c in chains:
            csb = [cs[(c * WINDOW_BLOCKS + b) * m:(c * WINDOW_BLOCKS + b + 1) * m, :]
                   for b in range(WINDOW_BLOCKS)]
            a = [jnp.where(strict, jnp.exp2(piece(z[c], 0) + csb[0]), 0.0)]
            carry = csb[0][:, 0:1]
            for b in range(1, WINDOW_BLOCKS):
                offset = jnp.where(js[c][b] >= 0, carry, NO_BLOCK_LOG2_WEIGHT)
                a.append(jnp.exp2((piece(z[c], b) + csb[b]) + offset))
                carry = carry + csb[b][:, 0:1]
            acc_ref[c] = _dot(jnp.concatenate(a, axis=1).astype(BF16),
                              jnp.concatenate([rows_of(v_ref, j) for j in jc[c]], axis=0))
            carries.append(carry)
        return carries

    def remainder(c, i, qs, carry):
        def more(state):
            j, _, live = state
            return jnp.logical_and(j >= 0, live)

        def step(state):
            j, carry, _ = state
            z = qk(qs, rows_of(k_ref, j))
            cs = _dot(hi_lo(softplus2(z)), tri)
            acc_ref[c] += _dot(jnp.exp2((z + cs) + carry).astype(BF16), rows_of(v_ref, j))
            carry = carry + cs[:, 0:1]
            return j - 1, carry, jnp.max(carry) > LOG2_WEIGHT_UNDERFLOW

        lax.while_loop(more, step, (i - WINDOW_BLOCKS, carry, jnp.max(carry) > LOG2_WEIGHT_UNDERFLOW))

    first = pl.program_id(2) * n_sub
    qss = []
    for c in range(n_sub):
        q = q_ref[0, c * kb:(c + 1) * kb, :]
        qss.append(jnp.concatenate(
            [jnp.where(lane_head == p, q, jnp.zeros_like(q)) for p in range(n_heads)], axis=0))
    carries = windows([first + c for c in range(n_sub)], qss)

    slowest = functools.reduce(jnp.maximum, carries)
    @pl.when(jnp.max(slowest) > LOG2_WEIGHT_UNDERFLOW)
    def _():
        for c in range(n_sub):
            remainder(c, first + c, qss[c], carries[c])

    for c in range(n_sub):
        out = acc_ref[c, 0:kb, :]
        for p in range(1, n_heads):
            out = jnp.where(lane_head == p, acc_ref[c, p * kb:(p + 1) * kb, :], out)
        o_ref[0, c * kb:(c + 1) * kb, :] = out.astype(BF16)


def _attention(q, k, v, tri, *, kb, lanes, n_sub):
    b, s, gw = q.shape
    blk = lambda bi, hi, qi: (bi, qi, hi)
    seq = lambda bi, hi, qi: (bi, 0, hi)
    return pl.pallas_call(
        functools.partial(_attn_kernel, kb=kb),
        grid=(b, gw // lanes, s // (n_sub * kb)),
        in_specs=[
            pl.BlockSpec((1, n_sub * kb, lanes), blk),
            pl.BlockSpec((1, s, lanes), seq),
            pl.BlockSpec((1, s, lanes), seq),
            pl.BlockSpec((2 * kb, kb), lambda bi, hi, qi: (0, 0)),
        ],
        out_specs=pl.BlockSpec((1, n_sub * kb, lanes), blk),
        out_shape=jax.ShapeDtypeStruct((b, s, gw), BF16),
        scratch_shapes=[pltpu.VMEM((n_sub, lanes // HEAD_DIM * kb, lanes), F32)],
        compiler_params=pltpu.CompilerParams(
            dimension_semantics=("parallel", "parallel", "arbitrary"),
            vmem_limit_bytes=VMEM_LIMIT_BYTES),
        name="sb_attn",
    )(q, k, v, tri)


def _mix_ffn_kernel(x_ref, osb_ref, u_ref, g_ref, wsp_ref, bias_ref, sbg_ref, sgg_ref, wout_ref,
                    fng_ref, wg_ref, wu_ref, wd_ref, o_ref, osgu_ref):
    tm, gw = u_ref.shape
    n_groups = wsp_ref.shape[0]
    ri = lax.broadcasted_iota(jnp.int32, (SPATIAL_CHUNK, SPATIAL_CHUNK), 0) // STREAM_CHUNK
    ci = lax.broadcasted_iota(jnp.int32, (SPATIAL_CHUNK, SPATIAL_CHUNK), 1) // STREAM_CHUNK
    w = [jnp.where(ci <= ri, wsp_ref[gi], 0.0).astype(BF16) for gi in range(n_groups)]
    low_half = lax.broadcasted_iota(jnp.int32, (SPATIAL_CHUNK, LANES), 1) < HEAD_DIM
    for c in range(tm // SPATIAL_CHUNK):
        r0 = c * SPATIAL_CHUNK
        for p in range(gw // LANES):
            l0 = p * LANES
            gp = g_ref[r0:r0 + SPATIAL_CHUNK, l0:l0 + LANES]
            s = jnp.where(low_half, _dot(w[2 * p], gp), _dot(w[2 * p + 1], gp))
            s = s + bias_ref[:, l0:l0 + LANES]
            osgu_ref[r0:r0 + SPATIAL_CHUNK, l0:l0 + LANES] = (
                u_ref[r0:r0 + SPATIAL_CHUNK, l0:l0 + LANES].astype(F32) * s)
    osgu = osgu_ref[...]
    sgn = (osgu * _rms_scale(osgu, gw) * sgg_ref[...]).astype(BF16)
    osb = osb_ref[...].astype(F32)
    sbn = (osb * _rms_scale(osb, gw) * sbg_ref[...]).astype(BF16)
    x1 = x_ref[...] + _dot(sbn, wout_ref[0:gw, :]) + _dot(sgn, wout_ref[gw:2 * gw, :])

    h = (x1 * _rms_scale(x1, x1.shape[-1]) * fng_ref[...]).astype(BF16)
    gate = _dot(h, wg_ref[...])
    up = _dot(h, wu_ref[...])
    ff = (gate * jax.nn.sigmoid(gate) * up).astype(BF16)
    o_ref[...] = x1 + _dot(ff, wd_ref[...])


def _mix_ffn(x2d, osb, u, g, w_spatial, bias, sb_gain, sgu_gain, w_out, ffn_gain, w_gate, w_up, w_down,
             *, tm):
    n, d = x2d.shape
    gw = u.shape[1]
    row = lambda i: (i, 0)
    const = lambda i: (0, 0)
    resident = functools.partial(pl.BlockSpec, index_map=const, pipeline_mode=pl.Buffered(1))
    return pl.pallas_call(
        _mix_ffn_kernel,
        grid=(n // tm,),
        in_specs=[
            pl.BlockSpec((tm, d), row),
            pl.BlockSpec((tm, gw), row),
            pl.BlockSpec((tm, gw), row),
            pl.BlockSpec((tm, gw), row),
            pl.BlockSpec(w_spatial.shape, lambda i: (0, 0, 0), pipeline_mode=pl.Buffered(1)),
            resident(bias.shape),
            pl.BlockSpec((1, gw), const),
            pl.BlockSpec((1, gw), const),
            resident(w_out.shape),
            pl.BlockSpec((1, d), const),
            resident(w_gate.shape),
            resident(w_up.shape),
            resident(w_down.shape),
        ],
        out_specs=pl.BlockSpec((tm, d), row),
        out_shape=jax.ShapeDtypeStruct((n, d), F32),
        scratch_shapes=[pltpu.VMEM((tm, gw), F32)],
        compiler_params=pltpu.CompilerParams(
            dimension_semantics=("parallel",), vmem_limit_bytes=VMEM_LIMIT_BYTES),
        name="mix_ffn",
    )(x2d, osb, u, g, w_spatial, bias, sb_gain, sgu_gain, w_out, ffn_gain, w_gate, w_up, w_down)


def _token_tile(s, preferred=512):
    tm = min(preferred, s)
    assert s % tm == 0 and tm % SPATIAL_CHUNK == 0
    return tm


def kernel(x, attn_norm_g, w_in, q_norm_g, k_norm_g, sgu_norm_g, w_spatial, b_spatial,
           sb_out_norm_g, sgu_out_norm_g, w_out, ffn_norm_g, w_gate, w_up, w_down):
    b, s, d = x.shape
    gw = w_in.shape[1] // 5
    n_heads = gw // HEAD_DIM
    assert gw % LANES == 0 and w_spatial.shape[1] == SPATIAL_CHUNK
    tm = _token_tile(s)
    kb = ATTN_BLOCK
    assert s % (ATTN_SUBBLOCKS * kb) == 0 and gw % ATTN_LANES == 0
    row = lambda a: a.reshape(1, -1).astype(F32)

    q_gain = row(jnp.tile(q_norm_g, n_heads) * (HEAD_DIM ** -0.5 * math.log2(math.e)))
    k_gain = row(jnp.tile(k_norm_g, n_heads))
    head_of = jnp.arange(gw) // HEAD_DIM
    gsum = (head_of[:, None] == head_of[None, :]).astype(BF16)
    pos = jnp.arange(kb)
    tri = jnp.tile(-(pos[:, None] >= pos[None, :]).astype(BF16), (2, 1))
    bias = jnp.repeat(b_spatial.T.astype(F32), HEAD_DIM, axis=1)

    x2d = x.reshape(b * s, d)
    q, k, v, u, g = _in_proj(x2d, row(attn_norm_g), w_in.astype(BF16), q_gain, k_gain,
                             row(sgu_norm_g), gsum, tm=tm)
    osb = _attention(q.reshape(b, s, gw), k.reshape(b, s, gw), v.reshape(b, s, gw), tri,
                     kb=kb, lanes=ATTN_LANES, n_sub=ATTN_SUBBLOCKS)
    y = _mix_ffn(x2d, osb.reshape(b * s, gw), u, g, w_spatial, bias, row(sb_out_norm_g),
                 row(sgu_out_norm_g), w_out.astype(BF16), row(ffn_norm_g), w_gate.astype(BF16),
                 w_up.astype(BF16), w_down.astype(BF16), tm=tm)
    return y.reshape(b, s, d)
```

```python
import functools
import math

import jax
import jax.numpy as jnp
from jax import lax
from jax.experimental import pallas as pl
from jax.experimental.pallas import tpu as pltpu

EPS = 1e-6
HEAD_DIM = 64
SPATIAL_CHUNK = 128
STREAM_CHUNK = 64
LANES = 128
MXU_DIM = 256
TOKEN_TILE = 512
ATTN_LANES = MXU_DIM
ATTN_BLOCK = LANES
ATTN_SUBBLOCKS = 2
WINDOW_BLOCKS = 3
FF_CHUNK = MXU_DIM
VMEM_LIMIT_BYTES = 60 * 1024 * 1024
LOG2_WEIGHT_UNDERFLOW = -150.0
NO_BLOCK_LOG2_WEIGHT = -1e30

F32 = jnp.float32
BF16 = jnp.bfloat16


def _rms_scale(v, width):
    return lax.rsqrt(jnp.sum(v * v, axis=-1, keepdims=True) * (1.0 / width) + EPS)


def _dot(a, b):
    return jnp.dot(a, b, preferred_element_type=F32)


def _in_proj_kernel(x_ref, ng_ref, w_ref, qg_ref, kg_ref, sg_ref, gsum_ref,
                    q_ref, k_ref, v_ref, u_ref, g_ref):
    gw = q_ref.shape[-1]
    x = x_ref[...]
    h = (x * _rms_scale(x, x.shape[-1]) * ng_ref[...]).astype(BF16)

    def head_norm(p, gain):
        ssq = _dot((p * p).astype(BF16), gsum_ref[...])
        return p * lax.rsqrt(ssq * (1.0 / HEAD_DIM) + EPS) * gain

    q_ref[...] = head_norm(_dot(h, w_ref[:, 0 * gw:1 * gw]), qg_ref[...]).astype(BF16)
    k_ref[...] = head_norm(_dot(h, w_ref[:, 1 * gw:2 * gw]), kg_ref[...]).astype(BF16)
    v_ref[...] = _dot(h, w_ref[:, 2 * gw:3 * gw]).astype(BF16)
    u_ref[...] = jax.nn.gelu(_dot(h, w_ref[:, 3 * gw:4 * gw])).astype(BF16)
    gg = jax.nn.gelu(_dot(h, w_ref[:, 4 * gw:5 * gw]))
    g_ref[...] = (gg * _rms_scale(gg, gw) * sg_ref[...]).astype(BF16)


def _in_proj(x2d, attn_norm_g, w_in, q_gain, k_gain, sgu_norm_g, gsum, *, tm):
    n, d = x2d.shape
    gw = w_in.shape[1] // 5
    row = lambda i: (i, 0)
    const = lambda i: (0, 0)
    out = jax.ShapeDtypeStruct((n, gw), BF16)
    return pl.pallas_call(
        _in_proj_kernel,
        grid=(n // tm,),
        in_specs=[
            pl.BlockSpec((tm, d), row),
            pl.BlockSpec((1, d), const),
            pl.BlockSpec((d, 5 * gw), const),
            pl.BlockSpec((1, gw), const),
            pl.BlockSpec((1, gw), const),
            pl.BlockSpec((1, gw), const),
            pl.BlockSpec((gw, gw), const),
        ],
        out_specs=[pl.BlockSpec((tm, gw), row)] * 5,
        out_shape=[out] * 5,
        compiler_params=pltpu.CompilerParams(
            dimension_semantics=("parallel",), vmem_limit_bytes=VMEM_LIMIT_BYTES),
        name="in_proj",
    )(x2d, attn_norm_g, w_in, q_gain, k_gain, sgu_norm_g, gsum)


def _attn_mix_ffn_kernel(q_ref, k_ref, v_ref, tri_ref, x_ref, u_ref, g_ref, wsp_ref, bias_ref,
                         sbg_ref, sgg_ref, wout_ref, fng_ref, wg_ref, wu_ref, wd_ref,
                         o_ref, osb_ref, acc_ref, osgu_ref, h_ref, *, tiles_per_seq):
    n = pl.program_id(0)
    tm, gw = u_ref.shape
    kb = ATTN_BLOCK
    n_groups_lanes = gw // ATTN_LANES
    n_qblocks = tm // kb
    n_heads = ATTN_LANES // HEAD_DIM
    m = n_heads * kb
    slot = n % 2
    first_block = (jnp.minimum(n, pl.num_programs(0) - 2) % tiles_per_seq) * n_qblocks

    @pl.when(n == 0)
    def _():
        osb_ref[1] = jnp.zeros(osb_ref.shape[1:], osb_ref.dtype)

    lane_head = lax.broadcasted_iota(jnp.int32, (kb, ATTN_LANES), 1) // HEAD_DIM
    strict = (lax.broadcasted_iota(jnp.int32, (m, kb), 1)
              < lax.broadcasted_iota(jnp.int32, (m, kb), 0) % kb)
    tri = tri_ref[...]
    piece = lambda t, b: t[:, b * kb:(b + 1) * kb]

    def rows_of(ref, j, lanes):
        return ref[0, pl.ds(pl.multiple_of(j * kb, kb), kb), lanes]

    def softplus2(z):
        return jnp.maximum(z, 0.0) + jnp.log2(1.0 + jnp.exp2(-jnp.abs(z)))

    def qk(qs, kmat):
        return lax.dot_general(qs, kmat, (((1,), (1,)), ((), ())), preferred_element_type=F32)

    def stacked_q(c, lanes):
        q = q_ref[0, c * kb:(c + 1) * kb, lanes]
        return jnp.concatenate(
            [jnp.where(lane_head == p, q, jnp.zeros_like(q)) for p in range(n_heads)], axis=0)

    def head_rows(pv):
        out = pv[0:kb, :]
        for p in range(1, n_heads):
            out = jnp.where(lane_head == p, pv[p * kb:(p + 1) * kb, :], out)
        return out

    carries = {}

    def window_stages(lanes, cs_):
        chains = range(len(cs_))
        js = [[first_block + c - b for b in range(WINDOW_BLOCKS)] for c in cs_]
        jc = [[jnp.maximum(j, 0) for j in row] for row in js]
        qss = [stacked_q(c, lanes) for c in cs_]
        z = [qk(qss[t], jnp.concatenate([rows_of(k_ref, j, lanes) for j in jc[t]], axis=0))
             for t in chains]
        yield
        sp = [softplus2(z[t]) for t in chains]
        spb = [(jnp.where(strict, piece(sp[t], 0), 0.0) if b == 0 else piece(sp[t], b)).astype(BF16)
               for t in chains for b in range(WINDOW_BLOCKS)]
        yield
        pairs = jnp.concatenate([jnp.concatenate(spb[e:e + 2], axis=1)
                                 for e in range(0, len(spb), 2)], axis=0)
        cs = _dot(pairs, tri)
        block_cs = lambda e: cs[(e // 2) * m:(e // 2 + 1) * m, (e % 2) * kb:(e % 2 + 1) * kb]
        weights = []
        for t in chains:
            csb = [block_cs(t * WINDOW_BLOCKS + b) for b in range(WINDOW_BLOCKS)]
            a = [jnp.where(strict, jnp.exp2(piece(z[t], 0) + csb[0]), 0.0)]
            carry = csb[0][:, 0:1]
            for b in range(1, WINDOW_BLOCKS):
                offset = jnp.where(js[t][b] >= 0, carry, NO_BLOCK_LOG2_WEIGHT)
                a.append(jnp.exp2((piece(z[t], b) + csb[b]) + offset))
                carry = carry + csb[b][:, 0:1]
            weights.append(jnp.concatenate(a, axis=1).astype(BF16))
            carries[(lanes.start, cs_[t])] = carry
        yield
        for t in chains:
            pv = _dot(weights[t], jnp.concatenate([rows_of(v_ref, j, lanes) for j in jc[t]], axis=0))
            acc_ref[cs_[t], :, lanes] = head_rows(pv)
        yield

    def remainder(lanes, c, carry):
        qs = stacked_q(c, lanes)

        def more(state):
            j, _, live = state
            return jnp.logical_and(j >= 0, live)

        def step(state):
            j, carry, _ = state
            z = qk(qs, rows_of(k_ref, j, lanes))
            cs = _dot(softplus2(z).astype(BF16), tri[0:kb, 0:kb])
            pv = _dot(jnp.exp2((z + cs) + carry).astype(BF16), rows_of(v_ref, j, lanes))
            acc_ref[c, :, lanes] += head_rows(pv)
            carry = carry + cs[:, 0:1]
            return j - 1, carry, jnp.max(carry) > LOG2_WEIGHT_UNDERFLOW

        lax.while_loop(more, step, (first_block + c - WINDOW_BLOCKS, carry,
                                    jnp.max(carry) > LOG2_WEIGHT_UNDERFLOW))

    def mix_ffn_stages():
        n_sgu = wsp_ref.shape[0]
        ri = lax.broadcasted_iota(jnp.int32, (SPATIAL_CHUNK, SPATIAL_CHUNK), 0) // STREAM_CHUNK
        ci = lax.broadcasted_iota(jnp.int32, (SPATIAL_CHUNK, SPATIAL_CHUNK), 1) // STREAM_CHUNK
        w = [jnp.where(ci <= ri, wsp_ref[gi], 0.0).astype(BF16) for gi in range(n_sgu)]
        low_half = lax.broadcasted_iota(jnp.int32, (SPATIAL_CHUNK, LANES), 1) < HEAD_DIM
        for p in range(gw // LANES):
            l0 = p * LANES
            wpair = jnp.concatenate([w[2 * p], w[2 * p + 1]], axis=1)
            for c in range(tm // SPATIAL_CHUNK):
                r0 = c * SPATIAL_CHUNK
                gp = g_ref[r0:r0 + SPATIAL_CHUNK, l0:l0 + LANES]
                zero = jnp.zeros_like(gp)
                gstack = jnp.concatenate(
                    [jnp.where(low_half, gp, zero), jnp.where(low_half, zero, gp)], axis=0)
                s = _dot(wpair, gstack) + bias_ref[:, l0:l0 + LANES]
                osgu_ref[r0:r0 + SPATIAL_CHUNK, l0:l0 + LANES] = (
                    u_ref[r0:r0 + SPATIAL_CHUNK, l0:l0 + LANES].astype(F32) * s)
        yield
        osgu = osgu_ref[...]
        sgn = (osgu * _rms_scale(osgu, gw) * sgg_ref[...]).astype(BF16)
        osb = osb_ref[1 - slot].astype(F32)
        sbn = (osb * _rms_scale(osb, gw) * sbg_ref[...]).astype(BF16)
        x1 = x_ref[...] + _dot(sbn, wout_ref[0:gw, :]) + _dot(sgn, wout_ref[gw:2 * gw, :])
        h_ref[...] = (x1 * _rms_scale(x1, x1.shape[-1]) * fng_ref[...]).astype(BF16)
        o_ref[...] = x1
        yield
        d_ff = wg_ref.shape[1]
        for f0 in range(0, d_ff, FF_CHUNK):
            h = h_ref[...]
            gate = _dot(h, wg_ref[:, f0:f0 + FF_CHUNK])
            up = _dot(h, wu_ref[:, f0:f0 + FF_CHUNK])
            ff = (gate * jax.nn.sigmoid(gate) * up).astype(BF16)
            o_ref[...] += _dot(ff, wd_ref[f0:f0 + FF_CHUNK, :])
            yield

    mixer = mix_ffn_stages()
    for hg in range(n_groups_lanes):
        lanes = slice(hg * ATTN_LANES, (hg + 1) * ATTN_LANES)
        for c0 in range(0, n_qblocks, ATTN_SUBBLOCKS):
            for _ in window_stages(lanes, list(range(c0, c0 + ATTN_SUBBLOCKS))):
                next(mixer, None)
    for _ in mixer:
        pass

    slowest = functools.reduce(jnp.maximum, carries.values())
    @pl.when(jnp.max(slowest) > LOG2_WEIGHT_UNDERFLOW)
    def _():
        for (l0, c), carry in carries.items():
            remainder(slice(l0, l0 + ATTN_LANES), c, carry)

    for c in range(n_qblocks):
        osb_ref[slot, c * kb:(c + 1) * kb, :] = acc_ref[c].astype(BF16)


def _attn_mix_ffn(q, k, v, tri, x2d, u, g, w_spatial, bias, sb_gain, sgu_gain, w_out, ffn_gain,
                  w_gate, w_up, w_down, *, tm):
    b, s, gw = q.shape
    n_tok, d = x2d.shape
    tiles_per_seq = s // tm
    n_tiles = n_tok // tm
    assert w_gate.shape[1] % FF_CHUNK == 0
    attn_tile = lambda n: jnp.minimum(n, n_tiles - 1)
    mix_tile = lambda n: jnp.maximum(n - 1, 0)
    const = lambda n: (0, 0)
    resident = functools.partial(pl.BlockSpec, index_map=const, pipeline_mode=pl.Buffered(1))
    sequence = functools.partial(
        pl.BlockSpec, (1, s, gw), lambda n: (attn_tile(n) // tiles_per_seq, 0, 0),
        pipeline_mode=pl.Buffered(1))
    return pl.pallas_call(
        functools.partial(_attn_mix_ffn_kernel, tiles_per_seq=tiles_per_seq),
        grid=(n_tiles + 1,),
        in_specs=[
            pl.BlockSpec((1, tm, gw),
                         lambda n: (attn_tile(n) // tiles_per_seq, attn_tile(n) % tiles_per_seq, 0)),
            sequence(),
            sequence(),
            resident(tri.shape),
            pl.BlockSpec((tm, d), lambda n: (mix_tile(n), 0)),
            pl.BlockSpec((tm, gw), lambda n: (mix_tile(n), 0)),
            pl.BlockSpec((tm, gw), lambda n: (mix_tile(n), 0)),
            pl.BlockSpec(w_spatial.shape, lambda n: (0, 0, 0), pipeline_mode=pl.Buffered(1)),
            resident(bias.shape),
            pl.BlockSpec((1, gw), const),
            pl.BlockSpec((1, gw), const),
            resident(w_out.shape),
            pl.BlockSpec((1, d), const),
            resident(w_gate.shape),
            resident(w_up.shape),
            resident(w_down.shape),
        ],
        out_specs=pl.BlockSpec((tm, d), lambda n: (mix_tile(n), 0)),
        out_shape=jax.ShapeDtypeStruct((n_tok, d), F32),
        scratch_shapes=[
            pltpu.VMEM((2, tm, gw), BF16),
            pltpu.VMEM((tm // ATTN_BLOCK, ATTN_BLOCK, gw), F32),
            pltpu.VMEM((tm, gw), F32),
            pltpu.VMEM((tm, d), BF16),
        ],
        compiler_params=pltpu.CompilerParams(
            dimension_semantics=("arbitrary",), vmem_limit_bytes=VMEM_LIMIT_BYTES),
        name="attn_mix_ffn",
    )(q, k, v, tri, x2d, u, g, w_spatial, bias, sb_gain, sgu_gain, w_out, ffn_gain,
      w_gate, w_up, w_down)


def kernel(x, attn_norm_g, w_in, q_norm_g, k_norm_g, sgu_norm_g, w_spatial, b_spatial,
           sb_out_norm_g, sgu_out_norm_g, w_out, ffn_norm_g, w_gate, w_up, w_down):
    b, s, d = x.shape
    gw = w_in.shape[1] // 5
    n_heads = gw // HEAD_DIM
    tm = TOKEN_TILE
    kb = ATTN_BLOCK
    assert w_spatial.shape[1] == SPATIAL_CHUNK and s % tm == 0 and gw % ATTN_LANES == 0
    assert tm % (ATTN_SUBBLOCKS * kb) == 0 and tm % SPATIAL_CHUNK == 0
    row = lambda a: a.reshape(1, -1).astype(F32)

    q_gain = row(jnp.tile(q_norm_g, n_heads) * (HEAD_DIM ** -0.5 * math.log2(math.e)))
    k_gain = row(jnp.tile(k_norm_g, n_heads))
    head_of = jnp.arange(gw) // HEAD_DIM
    gsum = (head_of[:, None] == head_of[None, :]).astype(BF16)
    pos = jnp.arange(kb)
    tri = jnp.kron(jnp.eye(2, dtype=BF16), -(pos[:, None] >= pos[None, :]).astype(BF16))
    bias = jnp.repeat(b_spatial.T.astype(F32), HEAD_DIM, axis=1)

    x2d = x.reshape(b * s, d)
    q, k, v, u, g = _in_proj(x2d, row(attn_norm_g), w_in.astype(BF16), q_gain, k_gain,
                             row(sgu_norm_g), gsum, tm=tm)
    y = _attn_mix_ffn(q.reshape(b, s, gw), k.reshape(b, s, gw), v.reshape(b, s, gw), tri, x2d, u, g,
                      w_spatial, bias, row(sb_out_norm_g), row(sgu_out_norm_g), w_out.astype(BF16),
                      row(ffn_norm_g), w_gate.astype(BF16), w_up.astype(BF16), w_down.astype(BF16),
                      tm=tm)
    return y.reshape(b, s, d)
```

```python
import functools
import math

import jax
import jax.numpy as jnp
from jax import lax
from jax.experimental import pallas as pl
from jax.experimental.pallas import tpu as pltpu

EPS = 1e-6
HEAD_DIM = 64
SPATIAL_CHUNK = 128
STREAM_CHUNK = 64
LANES = 128
MXU_DIM = 256
TOKEN_TILE = 512
ATTN_LANES = MXU_DIM
ATTN_BLOCK = LANES
ATTN_SUBBLOCKS = 2
NEAR_BLOCKS = 2
FF_CHUNK = MXU_DIM
VMEM_LIMIT_BYTES = 60 * 1024 * 1024
LOG2_WEIGHT_UNDERFLOW = -150.0
NO_BLOCK_LOG2_WEIGHT = -1e30

F32 = jnp.float32
BF16 = jnp.bfloat16


def _rms_scale(v, width):
    return lax.rsqrt(jnp.sum(v * v, axis=-1, keepdims=True) * (1.0 / width) + EPS)


def _dot(a, b):
    return jnp.dot(a, b, preferred_element_type=F32)


def _in_proj_kernel(x_ref, ng_ref, w_ref, qg_ref, kg_ref, sg_ref, gsum_ref,
                    q_ref, k_ref, v_ref, u_ref, g_ref):
    gw = q_ref.shape[-1]
    x = x_ref[...]
    h = (x * _rms_scale(x, x.shape[-1]) * ng_ref[...]).astype(BF16)

    def head_norm(p, gain):
        p2 = (p * p).astype(BF16)
        ssq = jnp.concatenate([_dot(p2[:, l0:l0 + MXU_DIM], gsum_ref[...])
                               for l0 in range(0, gw, MXU_DIM)], axis=1)
        return p * lax.rsqrt(ssq * (1.0 / HEAD_DIM) + EPS) * gain

    q_ref[...] = head_norm(_dot(h, w_ref[:, 0 * gw:1 * gw]), qg_ref[...]).astype(BF16)
    k_ref[...] = head_norm(_dot(h, w_ref[:, 1 * gw:2 * gw]), kg_ref[...]).astype(BF16)
    v_ref[...] = _dot(h, w_ref[:, 2 * gw:3 * gw]).astype(BF16)
    u_ref[...] = jax.nn.gelu(_dot(h, w_ref[:, 3 * gw:4 * gw])).astype(BF16)
    gg = jax.nn.gelu(_dot(h, w_ref[:, 4 * gw:5 * gw]))
    g_ref[...] = (gg * _rms_scale(gg, gw) * sg_ref[...]).astype(BF16)


def _in_proj(x2d, attn_norm_g, w_in, q_gain, k_gain, sgu_norm_g, gsum, *, tm):
    n, d = x2d.shape
    gw = w_in.shape[1] // 5
    row = lambda i: (i, 0)
    const = lambda i: (0, 0)
    out = jax.ShapeDtypeStruct((n, gw), BF16)
    return pl.pallas_call(
        _in_proj_kernel,
        grid=(n // tm,),
        in_specs=[
            pl.BlockSpec((tm, d), row),
            pl.BlockSpec((1, d), const),
            pl.BlockSpec((d, 5 * gw), const),
            pl.BlockSpec((1, gw), const),
            pl.BlockSpec((1, gw), const),
            pl.BlockSpec((1, gw), const),
            pl.BlockSpec(gsum.shape, const),
        ],
        out_specs=[pl.BlockSpec((tm, gw), row)] * 5,
        out_shape=[out] * 5,
        compiler_params=pltpu.CompilerParams(
            dimension_semantics=("parallel",), vmem_limit_bytes=VMEM_LIMIT_BYTES),
        name="in_proj",
    )(x2d, attn_norm_g, w_in, q_gain, k_gain, sgu_norm_g, gsum)


def _attn_mix_ffn_kernel(q_ref, k_ref, v_ref, tri_ref, x_ref, u_ref, g_ref, wsp_ref, bias_ref,
                         sbg_ref, sgg_ref, wout_ref, fng_ref, wg_ref, wu_ref, wd_ref,
                         o_ref, osb_ref, acc_ref, osgu_ref, h_ref, *, tiles_per_seq):
    n = pl.program_id(0)
    tm, gw = u_ref.shape
    kb = ATTN_BLOCK
    n_groups_lanes = gw // ATTN_LANES
    n_qblocks = tm // kb
    n_heads = ATTN_LANES // HEAD_DIM
    m = n_heads * kb
    half = kb // 2
    slot = n % 2
    first_block = (jnp.minimum(n, pl.num_programs(0) - 2) % tiles_per_seq) * n_qblocks

    @pl.when(n == 0)
    def _():
        osb_ref[1] = jnp.zeros(osb_ref.shape[1:], osb_ref.dtype)

    lane_head = lax.broadcasted_iota(jnp.int32, (kb, ATTN_LANES), 1) // HEAD_DIM
    strict = (lax.broadcasted_iota(jnp.int32, (m, kb), 1)
              < lax.broadcasted_iota(jnp.int32, (m, kb), 0) % kb)
    tri = tri_ref[...]
    piece = lambda t, b: t[:, b * kb:(b + 1) * kb]

    def rows_of(ref, j, lanes):
        return ref[0, pl.ds(pl.multiple_of(j * kb, kb), kb), lanes]

    def softplus2(z):
        return jnp.maximum(z, 0.0) + jnp.log2(1.0 + jnp.exp2(-jnp.abs(z)))

    def qk(qs, kmat):
        return lax.dot_general(qs, kmat, (((1,), (1,)), ((), ())), preferred_element_type=F32)

    def stacked_q(c, lanes):
        q = q_ref[0, c * kb:(c + 1) * kb, lanes]
        return jnp.concatenate(
            [jnp.where(lane_head == p, q, jnp.zeros_like(q)) for p in range(n_heads)], axis=0)

    def half_of(t, h):
        return jnp.concatenate([t[p * kb + h * half:p * kb + (h + 1) * half] for p in range(n_heads)],
                               axis=0)

    def head_rows(pv, rows):
        head = lax.broadcasted_iota(jnp.int32, (rows, ATTN_LANES), 1) // HEAD_DIM
        out = pv[0:rows, :]
        for p in range(1, n_heads):
            out = jnp.where(head == p, pv[p * rows:(p + 1) * rows, :], out)
        return out

    carries = {}

    def window_stages(lanes, cs_):
        chains = range(len(cs_))
        near = [[first_block + c - b for b in range(NEAR_BLOCKS)] for c in cs_]
        far = [first_block + c - NEAR_BLOCKS for c in cs_]
        k_near = [jnp.concatenate([rows_of(k_ref, jnp.maximum(j, 0), lanes) for j in near[t]], axis=0)
                  for t in chains]
        qss = [stacked_q(c, lanes) for c in cs_]
        z = [qk(qss[t], k_near[t]) for t in chains]
        zf = [qk(half_of(qss[t], 0), rows_of(k_ref, jnp.maximum(far[t], 0), lanes))
              for t in chains]
        yield
        sp = [softplus2(z[t]) for t in chains]
        spf = [softplus2(zf[t]) for t in chains]
        near_pairs = [jnp.concatenate([jnp.where(strict, piece(sp[t], 0), 0.0).astype(BF16),
                                       piece(sp[t], 1).astype(BF16)], axis=1) for t in chains]
        far_pair = jnp.concatenate([spf[t].astype(BF16) for t in chains], axis=1)
        yield
        cs = _dot(jnp.concatenate(near_pairs + [far_pair], axis=0), tri)
        weights = []
        for t in chains:
            cs0, cs1 = (cs[t * m:(t + 1) * m, b * kb:(b + 1) * kb] for b in range(NEAR_BLOCKS))
            csf = cs[len(cs_) * m:, t * kb:(t + 1) * kb]
            a0 = jnp.where(strict, jnp.exp2(piece(z[t], 0) + cs0), 0.0)
            a1 = jnp.exp2((piece(z[t], 1) + cs1)
                          + jnp.where(near[t][1] >= 0, cs0[:, 0:1], NO_BLOCK_LOG2_WEIGHT))
            upper, lower = (half_of(cs0, h)[:, 0:1] + half_of(cs1, h)[:, 0:1] for h in range(2))
            af = jnp.exp2((zf[t] + csf) + jnp.where(far[t] >= 0, upper, NO_BLOCK_LOG2_WEIGHT))
            weights.append((jnp.concatenate([a0, a1], axis=1).astype(BF16), af.astype(BF16)))
            carries[(lanes.start, cs_[t], 0)] = upper + csf[:, 0:1]
            carries[(lanes.start, cs_[t], 1)] = lower
        yield
        for t in chains:
            v_near = jnp.concatenate([rows_of(v_ref, jnp.maximum(j, 0), lanes) for j in near[t]], axis=0)
            acc_ref[cs_[t], :, lanes] = head_rows(_dot(weights[t][0], v_near), kb)
            acc_ref[cs_[t], 0:half, lanes] += head_rows(
                _dot(weights[t][1], rows_of(v_ref, jnp.maximum(far[t], 0), lanes)), half)
        yield

    def remainder(lanes, c, h, carry):
        qs = half_of(stacked_q(c, lanes), h)

        def more(state):
            j, _, live = state
            return jnp.logical_and(j >= 0, live)

        def step(state):
            j, carry, _ = state
            z = qk(qs, rows_of(k_ref, j, lanes))
            cs = _dot(softplus2(z).astype(BF16), tri[0:kb, 0:kb])
            pv = _dot(jnp.exp2((z + cs) + carry).astype(BF16), rows_of(v_ref, j, lanes))
            acc_ref[c, h * half:(h + 1) * half, lanes] += head_rows(pv, half)
            carry = carry + cs[:, 0:1]
            return j - 1, carry, jnp.max(carry) > LOG2_WEIGHT_UNDERFLOW

        first_unvisited = first_block + c - NEAR_BLOCKS - (1 - h)
        lax.while_loop(more, step, (first_unvisited, carry, jnp.max(carry) > LOG2_WEIGHT_UNDERFLOW))

    def mix_ffn_stages():
        n_sgu = wsp_ref.shape[0]
        ri = lax.broadcasted_iota(jnp.int32, (SPATIAL_CHUNK, SPATIAL_CHUNK), 0) // STREAM_CHUNK
        ci = lax.broadcasted_iota(jnp.int32, (SPATIAL_CHUNK, SPATIAL_CHUNK), 1) // STREAM_CHUNK
        w = [jnp.where(ci <= ri, wsp_ref[gi], 0.0).astype(BF16) for gi in range(n_sgu)]
        low_half = lax.broadcasted_iota(jnp.int32, (SPATIAL_CHUNK, LANES), 1) < HEAD_DIM
        for p in range(gw // LANES):
            l0 = p * LANES
            wpair = jnp.concatenate([w[2 * p], w[2 * p + 1]], axis=1)
            for c in range(tm // SPATIAL_CHUNK):
                r0 = c * SPATIAL_CHUNK
                gp = g_ref[r0:r0 + SPATIAL_CHUNK, l0:l0 + LANES]
                zero = jnp.zeros_like(gp)
                gstack = jnp.concatenate(
                    [jnp.where(low_half, gp, zero), jnp.where(low_half, zero, gp)], axis=0)
                s = _dot(wpair, gstack) + bias_ref[:, l0:l0 + LANES]
                osgu_ref[r0:r0 + SPATIAL_CHUNK, l0:l0 + LANES] = (
                    u_ref[r0:r0 + SPATIAL_CHUNK, l0:l0 + LANES].astype(F32) * s)
        yield
        osgu = osgu_ref[...]
        sgn = (osgu * _rms_scale(osgu, gw) * sgg_ref[...]).astype(BF16)
        osb = osb_ref[1 - slot].astype(F32)
        sbn = (osb * _rms_scale(osb, gw) * sbg_ref[...]).astype(BF16)
        x1 = x_ref[...] + _dot(sbn, wout_ref[0:gw, :]) + _dot(sgn, wout_ref[gw:2 * gw, :])
        h_ref[...] = (x1 * _rms_scale(x1, x1.shape[-1]) * fng_ref[...]).astype(BF16)
        o_ref[...] = x1
        yield
        d_ff = wg_ref.shape[1]
        for f0 in range(0, d_ff, FF_CHUNK):
            h = h_ref[...]
            gate = _dot(h, wg_ref[:, f0:f0 + FF_CHUNK])
            up = _dot(h, wu_ref[:, f0:f0 + FF_CHUNK])
            ff = (gate * jax.nn.sigmoid(gate) * up).astype(BF16)
            o_ref[...] += _dot(ff, wd_ref[f0:f0 + FF_CHUNK, :])
            yield

    mixer = mix_ffn_stages()
    for hg in range(n_groups_lanes):
        lanes = slice(hg * ATTN_LANES, (hg + 1) * ATTN_LANES)
        for c0 in range(0, n_qblocks, ATTN_SUBBLOCKS):
            for _ in window_stages(lanes, list(range(c0, c0 + ATTN_SUBBLOCKS))):
                next(mixer, None)
    for _ in mixer:
        pass

    slowest = functools.reduce(jnp.maximum, carries.values())
    @pl.when(jnp.max(slowest) > LOG2_WEIGHT_UNDERFLOW)
    def _():
        for (l0, c, h), carry in carries.items():
            remainder(slice(l0, l0 + ATTN_LANES), c, h, carry)

    for c in range(n_qblocks):
        osb_ref[slot, c * kb:(c + 1) * kb, :] = acc_ref[c].astype(BF16)


def _attn_mix_ffn(q, k, v, tri, x2d, u, g, w_spatial, bias, sb_gain, sgu_gain, w_out, ffn_gain,
                  w_gate, w_up, w_down, *, tm):
    b, s, gw = q.shape
    n_tok, d = x2d.shape
    tiles_per_seq = s // tm
    n_tiles = n_tok // tm
    assert w_gate.shape[1] % FF_CHUNK == 0
    attn_tile = lambda n: jnp.minimum(n, n_tiles - 1)
    mix_tile = lambda n: jnp.maximum(n - 1, 0)
    const = lambda n: (0, 0)
    resident = functools.partial(pl.BlockSpec, index_map=const, pipeline_mode=pl.Buffered(1))
    sequence = functools.partial(
        pl.BlockSpec, (1, s, gw), lambda n: (attn_tile(n) // tiles_per_seq, 0, 0),
        pipeline_mode=pl.Buffered(1))
    return pl.pallas_call(
        functools.partial(_attn_mix_ffn_kernel, tiles_per_seq=tiles_per_seq),
        grid=(n_tiles + 1,),
        in_specs=[
            pl.BlockSpec((1, tm, gw),
                         lambda n: (attn_tile(n) // tiles_per_seq, attn_tile(n) % tiles_per_seq, 0)),
            sequence(),
            sequence(),
            resident(tri.shape),
            pl.BlockSpec((tm, d), lambda n: (mix_tile(n), 0)),
            pl.BlockSpec((tm, gw), lambda n: (mix_tile(n), 0)),
            pl.BlockSpec((tm, gw), lambda n: (mix_tile(n), 0)),
            pl.BlockSpec(w_spatial.shape, lambda n: (0, 0, 0), pipeline_mode=pl.Buffered(1)),
            resident(bias.shape),
            pl.BlockSpec((1, gw), const),
            pl.BlockSpec((1, gw), const),
            resident(w_out.shape),
            pl.BlockSpec((1, d), const),
            resident(w_gate.shape),
            resident(w_up.shape),
            resident(w_down.shape),
        ],
        out_specs=pl.BlockSpec((tm, d), lambda n: (mix_tile(n), 0)),
        out_shape=jax.ShapeDtypeStruct((n_tok, d), F32),
        scratch_shapes=[
            pltpu.VMEM((2, tm, gw), BF16),
            pltpu.VMEM((tm // ATTN_BLOCK, ATTN_BLOCK, gw), F32),
            pltpu.VMEM((tm, gw), F32),
            pltpu.VMEM((tm, d), BF16),
        ],
        compiler_params=pltpu.CompilerParams(
            dimension_semantics=("arbitrary",), vmem_limit_bytes=VMEM_LIMIT_BYTES),
        name="attn_mix_ffn",
    )(q, k, v, tri, x2d, u, g, w_spatial, bias, sb_gain, sgu_gain, w_out, ffn_gain,
      w_gate, w_up, w_down)


def kernel(x, attn_norm_g, w_in, q_norm_g, k_norm_g, sgu_norm_g, w_spatial, b_spatial,
           sb_out_norm_g, sgu_out_norm_g, w_out, ffn_norm_g, w_gate, w_up, w_down):
    b, s, d = x.shape
    gw = w_in.shape[1] // 5
    n_heads = gw // HEAD_DIM
    tm = TOKEN_TILE
    kb = ATTN_BLOCK
    assert w_spatial.shape[1] == SPATIAL_CHUNK and s % tm == 0 and gw % ATTN_LANES == 0
    assert tm % (ATTN_SUBBLOCKS * kb) == 0 and tm % SPATIAL_CHUNK == 0
    row = lambda a: a.reshape(1, -1).astype(F32)

    q_gain = row(jnp.tile(q_norm_g, n_heads) * (HEAD_DIM ** -0.5 * math.log2(math.e)))
    k_gain = row(jnp.tile(k_norm_g, n_heads))
    head_of = jnp.arange(MXU_DIM) // HEAD_DIM
    gsum = (head_of[:, None] == head_of[None, :]).astype(BF16)
    pos = jnp.arange(kb)
    tri = jnp.kron(jnp.eye(2, dtype=BF16), -(pos[:, None] >= pos[None, :]).astype(BF16))
    bias = jnp.repeat(b_spatial.T.astype(F32), HEAD_DIM, axis=1)

    x2d = x.reshape(b * s, d)
    q, k, v, u, g = _in_proj(x2d, row(attn_norm_g), w_in.astype(BF16), q_gain, k_gain,
                             row(sgu_norm_g), gsum, tm=tm)
    y = _attn_mix_ffn(q.reshape(b, s, gw), k.reshape(b, s, gw), v.reshape(b, s, gw), tri, x2d, u, g,
                      w_spatial, bias, row(sb_out_norm_g), row(sgu_out_norm_g), w_out.astype(BF16),
                      row(ffn_norm_g), w_gate.astype(BF16), w_up.astype(BF16), w_down.astype(BF16),
                      tm=tm)
    return y.reshape(b, s, d)
```

```python
import functools
import math

import jax
import jax.numpy as jnp
from jax import lax
from jax.experimental import pallas as pl
from jax.experimental.pallas import tpu as pltpu

EPS = 1e-6
HEAD_DIM = 64
SPATIAL_CHUNK = 128
STREAM_CHUNK = 64
LANES = 128
MXU_DIM = 256
TOKEN_TILE = 512
ATTN_LANES = MXU_DIM
ATTN_BLOCK = LANES
ATTN_SUBBLOCKS = 2
NEAR_BLOCKS = 2
FF_CHUNK = MXU_DIM
EARLY_SCORE_WINDOWS = 2
VMEM_LIMIT_BYTES = 60 * 1024 * 1024
LOG2_WEIGHT_UNDERFLOW = -150.0
NO_BLOCK_LOG2_WEIGHT = -1e30

F32 = jnp.float32
BF16 = jnp.bfloat16


def _rms_scale(v, width):
    return lax.rsqrt(jnp.sum(v * v, axis=-1, keepdims=True) * (1.0 / width) + EPS)


def _dot(a, b):
    return jnp.dot(a, b, preferred_element_type=F32)


def _in_proj_kernel(x_ref, ng_ref, w_ref, qg_ref, kg_ref, sg_ref, gsum_ref,
                    q_ref, k_ref, v_ref, u_ref, g_ref):
    gw = q_ref.shape[-1]
    x = x_ref[...]
    h = (x * _rms_scale(x, x.shape[-1]) * ng_ref[...]).astype(BF16)

    def head_norm(p, gain):
        p2 = (p * p).astype(BF16)
        ssq = jnp.concatenate([_dot(p2[:, l0:l0 + MXU_DIM], gsum_ref[...])
                               for l0 in range(0, gw, MXU_DIM)], axis=1)
        return p * lax.rsqrt(ssq * (1.0 / HEAD_DIM) + EPS) * gain

    gg = jax.nn.gelu(_dot(h, w_ref[:, 4 * gw:5 * gw]))
    g_ref[...] = (gg * _rms_scale(gg, gw) * sg_ref[...]).astype(BF16)
    u_ref[...] = jax.nn.gelu(_dot(h, w_ref[:, 3 * gw:4 * gw])).astype(BF16)
    q_ref[...] = head_norm(_dot(h, w_ref[:, 0 * gw:1 * gw]), qg_ref[...]).astype(BF16)
    k_ref[...] = head_norm(_dot(h, w_ref[:, 1 * gw:2 * gw]), kg_ref[...]).astype(BF16)
    v_ref[...] = _dot(h, w_ref[:, 2 * gw:3 * gw]).astype(BF16)


def _in_proj(x2d, attn_norm_g, w_in, q_gain, k_gain, sgu_norm_g, gsum, *, tm):
    n, d = x2d.shape
    gw = w_in.shape[1] // 5
    row = lambda i: (i, 0)
    const = lambda i: (0, 0)
    out = jax.ShapeDtypeStruct((n, gw), BF16)
    return pl.pallas_call(
        _in_proj_kernel,
        grid=(n // tm,),
        in_specs=[
            pl.BlockSpec((tm, d), row),
            pl.BlockSpec((1, d), const),
            pl.BlockSpec((d, 5 * gw), const),
            pl.BlockSpec((1, gw), const),
            pl.BlockSpec((1, gw), const),
            pl.BlockSpec((1, gw), const),
            pl.BlockSpec(gsum.shape, const),
        ],
        out_specs=[pl.BlockSpec((tm, gw), row)] * 5,
        out_shape=[out] * 5,
        compiler_params=pltpu.CompilerParams(
            dimension_semantics=("parallel",), vmem_limit_bytes=VMEM_LIMIT_BYTES),
        name="in_proj",
    )(x2d, attn_norm_g, w_in, q_gain, k_gain, sgu_norm_g, gsum)


def _attn_mix_ffn_kernel(q_ref, k_ref, v_ref, tri_ref, x_ref, u_ref, g_ref, wsp_ref, bias_ref,
                         sbg_ref, sgg_ref, wout_ref, fng_ref, wg_ref, wu_ref, wd_ref,
                         o_ref, osb_ref, acc_ref, osgu_ref, h_ref, *, tiles_per_seq):
    n = pl.program_id(0)
    tm, gw = u_ref.shape
    kb = ATTN_BLOCK
    n_groups_lanes = gw // ATTN_LANES
    n_qblocks = tm // kb
    n_heads = ATTN_LANES // HEAD_DIM
    m = n_heads * kb
    half = kb // 2
    slot = n % 2
    first_block = (jnp.minimum(n, pl.num_programs(0) - 2) % tiles_per_seq) * n_qblocks

    @pl.when(n == 0)
    def _():
        osb_ref[1] = jnp.zeros(osb_ref.shape[1:], osb_ref.dtype)

    lane_head = lax.broadcasted_iota(jnp.int32, (kb, ATTN_LANES), 1) // HEAD_DIM
    strict = (lax.broadcasted_iota(jnp.int32, (m, kb), 1)
              < lax.broadcasted_iota(jnp.int32, (m, kb), 0) % kb)
    tri = tri_ref[...]
    piece = lambda t, b: t[:, b * kb:(b + 1) * kb]

    def rows_of(ref, j, lanes):
        return ref[0, pl.ds(pl.multiple_of(j * kb, kb), kb), lanes]

    def softplus2(z):
        return jnp.maximum(z, 0.0) + jnp.log2(1.0 + jnp.exp2(-jnp.abs(z)))

    def qk(qs, kmat):
        return lax.dot_general(qs, kmat, (((1,), (1,)), ((), ())), preferred_element_type=F32)

    def stacked_q(c, lanes):
        q = q_ref[0, c * kb:(c + 1) * kb, lanes]
        return jnp.concatenate(
            [jnp.where(lane_head == p, q, jnp.zeros_like(q)) for p in range(n_heads)], axis=0)

    def half_of(t, h):
        return jnp.concatenate([t[p * kb + h * half:p * kb + (h + 1) * half] for p in range(n_heads)],
                               axis=0)

    def head_rows(pv, rows):
        head = lax.broadcasted_iota(jnp.int32, (rows, ATTN_LANES), 1) // HEAD_DIM
        out = pv[0:rows, :]
        for p in range(1, n_heads):
            out = jnp.where(head == p, pv[p * rows:(p + 1) * rows, :], out)
        return out

    carries = {}

    def window_stages(lanes, cs_):
        chains = range(len(cs_))
        near = [[first_block + c - b for b in range(NEAR_BLOCKS)] for c in cs_]
        far = [first_block + c - NEAR_BLOCKS for c in cs_]
        k_near = [jnp.concatenate([rows_of(k_ref, jnp.maximum(j, 0), lanes) for j in near[t]], axis=0)
                  for t in chains]
        qss = [stacked_q(c, lanes) for c in cs_]
        z = [qk(qss[t], k_near[t]) for t in chains]
        zf = [qk(half_of(qss[t], 0), rows_of(k_ref, jnp.maximum(far[t], 0), lanes))
              for t in chains]
        yield
        sp = [softplus2(z[t]) for t in chains]
        spf = [softplus2(zf[t]) for t in chains]
        near_pairs = [jnp.concatenate([jnp.where(strict, piece(sp[t], 0), 0.0).astype(BF16),
                                       piece(sp[t], 1).astype(BF16)], axis=1) for t in chains]
        far_pair = jnp.concatenate([spf[t].astype(BF16) for t in chains], axis=1)
        yield
        cs = _dot(jnp.concatenate(near_pairs + [far_pair], axis=0), tri)
        weights = []
        for t in chains:
            cs0, cs1 = (cs[t * m:(t + 1) * m, b * kb:(b + 1) * kb] for b in range(NEAR_BLOCKS))
            csf = cs[len(cs_) * m:, t * kb:(t + 1) * kb]
            a0 = jnp.where(strict, jnp.exp2(piece(z[t], 0) + cs0), 0.0)
            a1 = jnp.exp2((piece(z[t], 1) + cs1)
                          + jnp.where(near[t][1] >= 0, cs0[:, 0:1], NO_BLOCK_LOG2_WEIGHT))
            upper, lower = (half_of(cs0, h)[:, 0:1] + half_of(cs1, h)[:, 0:1] for h in range(2))
            af = jnp.exp2((zf[t] + csf) + jnp.where(far[t] >= 0, upper, NO_BLOCK_LOG2_WEIGHT))
            weights.append((jnp.concatenate([a0, a1], axis=1).astype(BF16), af.astype(BF16)))
            carries[(lanes.start, cs_[t], 0)] = upper + csf[:, 0:1]
            carries[(lanes.start, cs_[t], 1)] = lower
        yield
        for t in chains:
            v_near = jnp.concatenate([rows_of(v_ref, jnp.maximum(j, 0), lanes) for j in near[t]], axis=0)
            acc_ref[cs_[t], :, lanes] = head_rows(_dot(weights[t][0], v_near), kb)
            acc_ref[cs_[t], 0:half, lanes] += head_rows(
                _dot(weights[t][1], rows_of(v_ref, jnp.maximum(far[t], 0), lanes)), half)
        yield

    def remainder(lanes, c, h, carry):
        qs = half_of(stacked_q(c, lanes), h)

        def more(state):
            j, _, live = state
            return jnp.logical_and(j >= 0, live)

        def step(state):
            j, carry, _ = state
            z = qk(qs, rows_of(k_ref, j, lanes))
            cs = _dot(softplus2(z).astype(BF16), tri[0:kb, 0:kb])
            pv = _dot(jnp.exp2((z + cs) + carry).astype(BF16), rows_of(v_ref, j, lanes))
            acc_ref[c, h * half:(h + 1) * half, lanes] += head_rows(pv, half)
            carry = carry + cs[:, 0:1]
            return j - 1, carry, jnp.max(carry) > LOG2_WEIGHT_UNDERFLOW

        first_unvisited = first_block + c - NEAR_BLOCKS - (1 - h)
        lax.while_loop(more, step, (first_unvisited, carry, jnp.max(carry) > LOG2_WEIGHT_UNDERFLOW))

    def mix_ffn_stages():
        n_sgu = wsp_ref.shape[0]
        ri = lax.broadcasted_iota(jnp.int32, (SPATIAL_CHUNK, SPATIAL_CHUNK), 0) // STREAM_CHUNK
        ci = lax.broadcasted_iota(jnp.int32, (SPATIAL_CHUNK, SPATIAL_CHUNK), 1) // STREAM_CHUNK
        w = [jnp.where(ci <= ri, wsp_ref[gi], 0.0).astype(BF16) for gi in range(n_sgu)]
        low_half = lax.broadcasted_iota(jnp.int32, (SPATIAL_CHUNK, LANES), 1) < HEAD_DIM
        for p in range(gw // LANES):
            l0 = p * LANES
            wpair = jnp.concatenate([w[2 * p], w[2 * p + 1]], axis=1)
            for c in range(tm // SPATIAL_CHUNK):
                r0 = c * SPATIAL_CHUNK
                gp = g_ref[r0:r0 + SPATIAL_CHUNK, l0:l0 + LANES]
                zero = jnp.zeros_like(gp)
                gstack = jnp.concatenate(
                    [jnp.where(low_half, gp, zero), jnp.where(low_half, zero, gp)], axis=0)
                s = _dot(wpair, gstack) + bias_ref[:, l0:l0 + LANES]
                osgu_ref[r0:r0 + SPATIAL_CHUNK, l0:l0 + LANES] = (
                    u_ref[r0:r0 + SPATIAL_CHUNK, l0:l0 + LANES].astype(F32) * s)
        yield
        osgu = osgu_ref[...]
        sgn = (osgu * _rms_scale(osgu, gw) * sgg_ref[...]).astype(BF16)
        osb = osb_ref[1 - slot].astype(F32)
        sbn = (osb * _rms_scale(osb, gw) * sbg_ref[...]).astype(BF16)
        x1 = x_ref[...] + _dot(sbn, wout_ref[0:gw, :]) + _dot(sgn, wout_ref[gw:2 * gw, :])
        h_ref[...] = (x1 * _rms_scale(x1, x1.shape[-1]) * fng_ref[...]).astype(BF16)
        o_ref[...] = x1
        yield
        def gate_up(f0):
            h = h_ref[...]
            return _dot(h, wg_ref[:, f0:f0 + FF_CHUNK]), _dot(h, wu_ref[:, f0:f0 + FF_CHUNK])

        chunks = list(range(0, wg_ref.shape[1], FF_CHUNK))
        projected = gate_up(chunks[0])
        for i, f0 in enumerate(chunks):
            gate, up = projected
            if i + 1 < len(chunks):
                projected = gate_up(chunks[i + 1])
            ff = (gate * jax.nn.sigmoid(gate) * up).astype(BF16)
            o_ref[...] += _dot(ff, wd_ref[f0:f0 + FF_CHUNK, :])
            yield

    mixer = mix_ffn_stages()
    windows = [window_stages(slice(hg * ATTN_LANES, (hg + 1) * ATTN_LANES),
                             list(range(c0, c0 + ATTN_SUBBLOCKS)))
               for hg in range(n_groups_lanes) for c0 in range(0, n_qblocks, ATTN_SUBBLOCKS)]
    for win in windows[:EARLY_SCORE_WINDOWS]:
        next(win)
        next(mixer, None)
    for win in windows:
        for _ in win:
            next(mixer, None)
    for _ in mixer:
        pass

    slowest = functools.reduce(jnp.maximum, carries.values())
    @pl.when(jnp.max(slowest) > LOG2_WEIGHT_UNDERFLOW)
    def _():
        for (l0, c, h), carry in carries.items():
            remainder(slice(l0, l0 + ATTN_LANES), c, h, carry)

    for c in range(n_qblocks):
        osb_ref[slot, c * kb:(c + 1) * kb, :] = acc_ref[c].astype(BF16)


def _attn_mix_ffn(q, k, v, tri, x2d, u, g, w_spatial, bias, sb_gain, sgu_gain, w_out, ffn_gain,
                  w_gate, w_up, w_down, *, tm):
    b, s, gw = q.shape
    n_tok, d = x2d.shape
    tiles_per_seq = s // tm
    n_tiles = n_tok // tm
    assert w_gate.shape[1] % FF_CHUNK == 0
    attn_tile = lambda n: jnp.minimum(n, n_tiles - 1)
    mix_tile = lambda n: jnp.maximum(n - 1, 0)
    const = lambda n: (0, 0)
    resident = functools.partial(pl.BlockSpec, index_map=const, pipeline_mode=pl.Buffered(1))
    sequence = functools.partial(
        pl.BlockSpec, (1, s, gw), lambda n: (attn_tile(n) // tiles_per_seq, 0, 0),
        pipeline_mode=pl.Buffered(1))
    return pl.pallas_call(
        functools.partial(_attn_mix_ffn_kernel, tiles_per_seq=tiles_per_seq),
        grid=(n_tiles + 1,),
        in_specs=[
            pl.BlockSpec((1, tm, gw),
                         lambda n: (attn_tile(n) // tiles_per_seq, attn_tile(n) % tiles_per_seq, 0)),
            sequence(),
            sequence(),
            resident(tri.shape),
            pl.BlockSpec((tm, d), lambda n: (mix_tile(n), 0)),
            pl.BlockSpec((tm, gw), lambda n: (mix_tile(n), 0)),
            pl.BlockSpec((tm, gw), lambda n: (mix_tile(n), 0)),
            pl.BlockSpec(w_spatial.shape, lambda n: (0, 0, 0), pipeline_mode=pl.Buffered(1)),
            resident(bias.shape),
            pl.BlockSpec((1, gw), const),
            pl.BlockSpec((1, gw), const),
            resident(w_out.shape),
            pl.BlockSpec((1, d), const),
            resident(w_gate.shape),
            resident(w_up.shape),
            resident(w_down.shape),
        ],
        out_specs=pl.BlockSpec((tm, d), lambda n: (mix_tile(n), 0)),
        out_shape=jax.ShapeDtypeStruct((n_tok, d), F32),
        scratch_shapes=[
            pltpu.VMEM((2, tm, gw), BF16),
            pltpu.VMEM((tm // ATTN_BLOCK, ATTN_BLOCK, gw), F32),
            pltpu.VMEM((tm, gw), F32),
            pltpu.VMEM((tm, d), BF16),
        ],
        compiler_params=pltpu.CompilerParams(
            dimension_semantics=("arbitrary",), vmem_limit_bytes=VMEM_LIMIT_BYTES),
        name="attn_mix_ffn",
    )(q, k, v, tri, x2d, u, g, w_spatial, bias, sb_gain, sgu_gain, w_out, ffn_gain,
      w_gate, w_up, w_down)


def kernel(x, attn_norm_g, w_in, q_norm_g, k_norm_g, sgu_norm_g, w_spatial, b_spatial,
           sb_out_norm_g, sgu_out_norm_g, w_out, ffn_norm_g, w_gate, w_up, w_down):
    b, s, d = x.shape
    gw = w_in.shape[1] // 5
    n_heads = gw // HEAD_DIM
    tm = TOKEN_TILE
    kb = ATTN_BLOCK
    assert w_spatial.shape[1] == SPATIAL_CHUNK and s % tm == 0 and gw % ATTN_LANES == 0
    assert tm % (ATTN_SUBBLOCKS * kb) == 0 and tm % SPATIAL_CHUNK == 0
    row = lambda a: a.reshape(1, -1).astype(F32)

    q_gain = row(jnp.tile(q_norm_g, n_heads) * (HEAD_DIM ** -0.5 * math.log2(math.e)))
    k_gain = row(jnp.tile(k_norm_g, n_heads))
    head_of = jnp.arange(MXU_DIM) // HEAD_DIM
    gsum = (head_of[:, None] == head_of[None, :]).astype(BF16)
    pos = jnp.arange(kb)
    tri = jnp.kron(jnp.eye(2, dtype=BF16), -(pos[:, None] >= pos[None, :]).astype(BF16))
    bias = jnp.repeat(b_spatial.T.astype(F32), HEAD_DIM, axis=1)

    x2d = x.reshape(b * s, d)
    q, k, v, u, g = _in_proj(x2d, row(attn_norm_g), w_in.astype(BF16), q_gain, k_gain,
                             row(sgu_norm_g), gsum, tm=tm)
    y = _attn_mix_ffn(q.reshape(b, s, gw), k.reshape(b, s, gw), v.reshape(b, s, gw), tri, x2d, u, g,
                      w_spatial, bias, row(sb_out_norm_g), row(sgu_out_norm_g), w_out.astype(BF16),
                      row(ffn_norm_g), w_gate.astype(BF16), w_up.astype(BF16), w_down.astype(BF16),
                      tm=tm)
    return y.reshape(b, s, d)
```

```python
import functools
import math

import jax
import jax.numpy as jnp
from jax import lax
from jax.experimental import pallas as pl
from jax.experimental.pallas import tpu as pltpu

EPS = 1e-6
HEAD_DIM = 64
SPATIAL_CHUNK = 128
STREAM_CHUNK = 64
LANES = 128
MXU_DIM = 256
TOKEN_TILE = 512
IN_PROJ_TILE = 1024
ATTN_LANES = MXU_DIM
ATTN_BLOCK = LANES
ATTN_SUBBLOCKS = 2
NEAR_BLOCKS = 2
FF_CHUNK = MXU_DIM
EARLY_SCORE_WINDOWS = 2
VMEM_LIMIT_BYTES = 60 * 1024 * 1024
LOG2_WEIGHT_UNDERFLOW = -150.0
NO_BLOCK_LOG2_WEIGHT = -1e30

F32 = jnp.float32
BF16 = jnp.bfloat16


def _rms_scale(v, width):
    return lax.rsqrt(jnp.sum(v * v, axis=-1, keepdims=True) * (1.0 / width) + EPS)


def _dot(a, b):
    return jnp.dot(a, b, preferred_element_type=F32)


def _in_proj_kernel(x_ref, ng_ref, w_ref, qg_ref, kg_ref, sg_ref, gsum_ref,
                    q_ref, k_ref, v_ref, u_ref, g_ref):
    gw = q_ref.shape[-1]
    x = x_ref[...]
    h = (x * _rms_scale(x, x.shape[-1]) * ng_ref[...]).astype(BF16)

    def head_norm(p, gain):
        p2 = (p * p).astype(BF16)
        ssq = jnp.concatenate([_dot(p2[:, l0:l0 + MXU_DIM], gsum_ref[...])
                               for l0 in range(0, gw, MXU_DIM)], axis=1)
        return p * lax.rsqrt(ssq * (1.0 / HEAD_DIM) + EPS) * gain

    gg = jax.nn.gelu(_dot(h, w_ref[:, 4 * gw:5 * gw]))
    g_ref[...] = (gg * _rms_scale(gg, gw) * sg_ref[...]).astype(BF16)
    u_ref[...] = jax.nn.gelu(_dot(h, w_ref[:, 3 * gw:4 * gw])).astype(BF16)
    q_ref[...] = head_norm(_dot(h, w_ref[:, 0 * gw:1 * gw]), qg_ref[...]).astype(BF16)
    k_ref[...] = head_norm(_dot(h, w_ref[:, 1 * gw:2 * gw]), kg_ref[...]).astype(BF16)
    v_ref[...] = _dot(h, w_ref[:, 2 * gw:3 * gw]).astype(BF16)


def _in_proj(x2d, attn_norm_g, w_in, q_gain, k_gain, sgu_norm_g, gsum, *, tm):
    n, d = x2d.shape
    gw = w_in.shape[1] // 5
    row = lambda i: (i, 0)
    const = lambda i: (0, 0)
    out = jax.ShapeDtypeStruct((n, gw), BF16)
    return pl.pallas_call(
        _in_proj_kernel,
        grid=(n // tm,),
        in_specs=[
            pl.BlockSpec((tm, d), row),
            pl.BlockSpec((1, d), const),
            pl.BlockSpec((d, 5 * gw), const, pipeline_mode=pl.Buffered(1)),
            pl.BlockSpec((1, gw), const),
            pl.BlockSpec((1, gw), const),
            pl.BlockSpec((1, gw), const),
            pl.BlockSpec(gsum.shape, const),
        ],
        out_specs=[pl.BlockSpec((tm, gw), row)] * 5,
        out_shape=[out] * 5,
        compiler_params=pltpu.CompilerParams(
            dimension_semantics=("parallel",), vmem_limit_bytes=VMEM_LIMIT_BYTES),
        name="in_proj",
    )(x2d, attn_norm_g, w_in, q_gain, k_gain, sgu_norm_g, gsum)


def _attn_mix_ffn_kernel(q_ref, kt_ref, vt_ref, tri_ref, x_ref, u_ref, g_ref, wsp_ref, bias_ref,
                         sbg_ref, sgg_ref, wout_ref, fng_ref, wg_ref, wu_ref, wd_ref,
                         o_ref, k_ref, v_ref, osb_ref, acc_ref, osgu_ref, h_ref, *, tiles_per_seq):
    n = pl.program_id(0)
    tm, gw = u_ref.shape
    kb = ATTN_BLOCK
    n_groups_lanes = gw // ATTN_LANES
    n_qblocks = tm // kb
    n_heads = ATTN_LANES // HEAD_DIM
    m = n_heads * kb
    half = kb // 2
    slot = n % 2
    first_block = (jnp.minimum(n, pl.num_programs(0) - 2) % tiles_per_seq) * n_qblocks

    @pl.when(n == 0)
    def _():
        osb_ref[1] = jnp.zeros(osb_ref.shape[1:], osb_ref.dtype)

    tile_rows = pl.ds(pl.multiple_of(first_block * kb, tm), tm)
    k_ref[tile_rows, :] = kt_ref[0]
    v_ref[tile_rows, :] = vt_ref[0]

    lane_head = lax.broadcasted_iota(jnp.int32, (kb, ATTN_LANES), 1) // HEAD_DIM
    strict = (lax.broadcasted_iota(jnp.int32, (m, kb), 1)
              < lax.broadcasted_iota(jnp.int32, (m, kb), 0) % kb)
    tri = tri_ref[...]
    piece = lambda t, b: t[:, b * kb:(b + 1) * kb]

    def rows_of(ref, j, lanes):
        return ref[pl.ds(pl.multiple_of(j * kb, kb), kb), lanes]

    def softplus2(z):
        return jnp.maximum(z, 0.0) + jnp.log2(1.0 + jnp.exp2(-jnp.abs(z)))

    def qk(qs, kmat):
        return lax.dot_general(qs, kmat, (((1,), (1,)), ((), ())), preferred_element_type=F32)

    def stacked_q(c, lanes):
        q = q_ref[0, c * kb:(c + 1) * kb, lanes]
        return jnp.concatenate(
            [jnp.where(lane_head == p, q, jnp.zeros_like(q)) for p in range(n_heads)], axis=0)

    def half_of(t, h):
        return jnp.concatenate([t[p * kb + h * half:p * kb + (h + 1) * half] for p in range(n_heads)],
                               axis=0)

    def head_rows(pv, rows):
        head = lax.broadcasted_iota(jnp.int32, (rows, ATTN_LANES), 1) // HEAD_DIM
        out = pv[0:rows, :]
        for p in range(1, n_heads):
            out = jnp.where(head == p, pv[p * rows:(p + 1) * rows, :], out)
        return out

    carries = {}

    def window_stages(lanes, cs_):
        chains = range(len(cs_))
        near = [[first_block + c - b for b in range(NEAR_BLOCKS)] for c in cs_]
        far = [first_block + c - NEAR_BLOCKS for c in cs_]
        k_near = [jnp.concatenate([rows_of(k_ref, jnp.maximum(j, 0), lanes) for j in near[t]], axis=0)
                  for t in chains]
        qss = [stacked_q(c, lanes) for c in cs_]
        z = [qk(qss[t], k_near[t]) for t in chains]
        zf = [qk(half_of(qss[t], 0), rows_of(k_ref, jnp.maximum(far[t], 0), lanes))
              for t in chains]
        yield
        sp = [softplus2(z[t]) for t in chains]
        spf = [softplus2(zf[t]) for t in chains]
        near_pairs = [jnp.concatenate([jnp.where(strict, piece(sp[t], 0), 0.0).astype(BF16),
                                       piece(sp[t], 1).astype(BF16)], axis=1) for t in chains]
        far_pair = jnp.concatenate([spf[t].astype(BF16) for t in chains], axis=1)
        yield
        cs = _dot(jnp.concatenate(near_pairs + [far_pair], axis=0), tri)
        weights = []
        for t in chains:
            cs0, cs1 = (cs[t * m:(t + 1) * m, b * kb:(b + 1) * kb] for b in range(NEAR_BLOCKS))
            csf = cs[len(cs_) * m:, t * kb:(t + 1) * kb]
            a0 = jnp.where(strict, jnp.exp2(piece(z[t], 0) + cs0), 0.0)
            a1 = jnp.exp2((piece(z[t], 1) + cs1)
                          + jnp.where(near[t][1] >= 0, cs0[:, 0:1], NO_BLOCK_LOG2_WEIGHT))
            upper, lower = (half_of(cs0, h)[:, 0:1] + half_of(cs1, h)[:, 0:1] for h in range(2))
            af = jnp.exp2((zf[t] + csf) + jnp.where(far[t] >= 0, upper, NO_BLOCK_LOG2_WEIGHT))
            weights.append((jnp.concatenate([a0, a1], axis=1).astype(BF16), af.astype(BF16)))
            carries[(lanes.start, cs_[t], 0)] = upper + csf[:, 0:1]
            carries[(lanes.start, cs_[t], 1)] = lower
        yield
        for t in chains:
            v_near = jnp.concatenate([rows_of(v_ref, jnp.maximum(j, 0), lanes) for j in near[t]], axis=0)
            acc_ref[cs_[t], :, lanes] = head_rows(_dot(weights[t][0], v_near), kb)
            acc_ref[cs_[t], 0:half, lanes] += head_rows(
                _dot(weights[t][1], rows_of(v_ref, jnp.maximum(far[t], 0), lanes)), half)
        yield

    def remainder(lanes, c, h, carry):
        qs = half_of(stacked_q(c, lanes), h)

        def more(state):
            j, _, live = state
            return jnp.logical_and(j >= 0, live)

        def step(state):
            j, carry, _ = state
            z = qk(qs, rows_of(k_ref, j, lanes))
            cs = _dot(softplus2(z).astype(BF16), tri[0:kb, 0:kb])
            pv = _dot(jnp.exp2((z + cs) + carry).astype(BF16), rows_of(v_ref, j, lanes))
            acc_ref[c, h * half:(h + 1) * half, lanes] += head_rows(pv, half)
            carry = carry + cs[:, 0:1]
            return j - 1, carry, jnp.max(carry) > LOG2_WEIGHT_UNDERFLOW

        first_unvisited = first_block + c - NEAR_BLOCKS - (1 - h)
        lax.while_loop(more, step, (first_unvisited, carry, jnp.max(carry) > LOG2_WEIGHT_UNDERFLOW))

    def mix_ffn_stages():
        n_sgu = wsp_ref.shape[0]
        ri = lax.broadcasted_iota(jnp.int32, (SPATIAL_CHUNK, SPATIAL_CHUNK), 0) // STREAM_CHUNK
        ci = lax.broadcasted_iota(jnp.int32, (SPATIAL_CHUNK, SPATIAL_CHUNK), 1) // STREAM_CHUNK
        w = [jnp.where(ci <= ri, wsp_ref[gi], 0.0).astype(BF16) for gi in range(n_sgu)]
        low_half = lax.broadcasted_iota(jnp.int32, (SPATIAL_CHUNK, LANES), 1) < HEAD_DIM
        for p in range(gw // LANES):
            l0 = p * LANES
            wpair = jnp.concatenate([w[2 * p], w[2 * p + 1]], axis=1)
            for c in range(tm // SPATIAL_CHUNK):
                r0 = c * SPATIAL_CHUNK
                gp = g_ref[r0:r0 + SPATIAL_CHUNK, l0:l0 + LANES]
                zero = jnp.zeros_like(gp)
                gstack = jnp.concatenate(
                    [jnp.where(low_half, gp, zero), jnp.where(low_half, zero, gp)], axis=0)
                s = _dot(wpair, gstack) + bias_ref[:, l0:l0 + LANES]
                osgu_ref[r0:r0 + SPATIAL_CHUNK, l0:l0 + LANES] = (
                    u_ref[r0:r0 + SPATIAL_CHUNK, l0:l0 + LANES].astype(F32) * s)
        yield
        osgu = osgu_ref[...]
        sgn = (osgu * _rms_scale(osgu, gw) * sgg_ref[...]).astype(BF16)
        osb = osb_ref[1 - slot].astype(F32)
        sbn = (osb * _rms_scale(osb, gw) * sbg_ref[...]).astype(BF16)
        x1 = x_ref[...] + _dot(sbn, wout_ref[0:gw, :]) + _dot(sgn, wout_ref[gw:2 * gw, :])
        h_ref[...] = (x1 * _rms_scale(x1, x1.shape[-1]) * fng_ref[...]).astype(BF16)
        o_ref[...] = x1
        yield
        def gate_up(cols):
            h = h_ref[...]
            return _dot(h, wg_ref[:, cols]), _dot(h, wu_ref[:, cols])

        d_ff = wg_ref.shape[1]
        chunks = [slice(f0, min(f0 + FF_CHUNK, d_ff)) for f0 in range(0, d_ff, FF_CHUNK)]
        projected = gate_up(chunks[0])
        for i, cols in enumerate(chunks):
            gate, up = projected
            if i + 1 < len(chunks):
                projected = gate_up(chunks[i + 1])
            ff = (gate * jax.nn.sigmoid(gate) * up).astype(BF16)
            o_ref[...] += _dot(ff, wd_ref[cols, :])
            yield

    mixer = mix_ffn_stages()
    windows = [window_stages(slice(hg * ATTN_LANES, (hg + 1) * ATTN_LANES),
                             list(range(c0, c0 + ATTN_SUBBLOCKS)))
               for hg in range(n_groups_lanes) for c0 in range(0, n_qblocks, ATTN_SUBBLOCKS)]
    for win in windows[:EARLY_SCORE_WINDOWS]:
        next(win)
        next(mixer, None)
    for win in windows:
        for _ in win:
            next(mixer, None)
    for _ in mixer:
        pass

    slowest = functools.reduce(jnp.maximum, carries.values())
    @pl.when(jnp.max(slowest) > LOG2_WEIGHT_UNDERFLOW)
    def _():
        for (l0, c, h), carry in carries.items():
            remainder(slice(l0, l0 + ATTN_LANES), c, h, carry)

    for c in range(n_qblocks):
        osb_ref[slot, c * kb:(c + 1) * kb, :] = acc_ref[c].astype(BF16)


def _attn_mix_ffn(q, k, v, tri, x2d, u, g, w_spatial, bias, sb_gain, sgu_gain, w_out, ffn_gain,
                  w_gate, w_up, w_down, *, tm):
    b, s, gw = q.shape
    n_tok, d = x2d.shape
    tiles_per_seq = s // tm
    n_tiles = n_tok // tm
    assert w_gate.shape[1] % MXU_DIM == 0 and FF_CHUNK % MXU_DIM == 0
    attn_tile = lambda n: jnp.minimum(n, n_tiles - 1)
    mix_tile = lambda n: jnp.maximum(n - 1, 0)
    const = lambda n: (0, 0)
    resident = functools.partial(pl.BlockSpec, index_map=const, pipeline_mode=pl.Buffered(1))
    attn_block = pl.BlockSpec(
        (1, tm, gw), lambda n: (attn_tile(n) // tiles_per_seq, attn_tile(n) % tiles_per_seq, 0))
    return pl.pallas_call(
        functools.partial(_attn_mix_ffn_kernel, tiles_per_seq=tiles_per_seq),
        grid=(n_tiles + 1,),
        in_specs=[
            attn_block,
            attn_block,
            attn_block,
            resident(tri.shape),
            pl.BlockSpec((tm, d), lambda n: (mix_tile(n), 0)),
            pl.BlockSpec((tm, gw), lambda n: (mix_tile(n), 0)),
            pl.BlockSpec((tm, gw), lambda n: (mix_tile(n), 0)),
            pl.BlockSpec(w_spatial.shape, lambda n: (0, 0, 0), pipeline_mode=pl.Buffered(1)),
            resident(bias.shape),
            pl.BlockSpec((1, gw), const),
            pl.BlockSpec((1, gw), const),
            resident(w_out.shape),
            pl.BlockSpec((1, d), const),
            resident(w_gate.shape),
            resident(w_up.shape),
            resident(w_down.shape),
        ],
        out_specs=pl.BlockSpec((tm, d), lambda n: (mix_tile(n), 0)),
        out_shape=jax.ShapeDtypeStruct((n_tok, d), F32),
        scratch_shapes=[
            pltpu.VMEM((s, gw), BF16),
            pltpu.VMEM((s, gw), BF16),
            pltpu.VMEM((2, tm, gw), BF16),
            pltpu.VMEM((tm // ATTN_BLOCK, ATTN_BLOCK, gw), F32),
            pltpu.VMEM((tm, gw), F32),
            pltpu.VMEM((tm, d), BF16),
        ],
        compiler_params=pltpu.CompilerParams(
            dimension_semantics=("arbitrary",), vmem_limit_bytes=VMEM_LIMIT_BYTES),
        name="attn_mix_ffn",
    )(q, k, v, tri, x2d, u, g, w_spatial, bias, sb_gain, sgu_gain, w_out, ffn_gain,
      w_gate, w_up, w_down)


def kernel(x, attn_norm_g, w_in, q_norm_g, k_norm_g, sgu_norm_g, w_spatial, b_spatial,
           sb_out_norm_g, sgu_out_norm_g, w_out, ffn_norm_g, w_gate, w_up, w_down):
    b, s, d = x.shape
    gw = w_in.shape[1] // 5
    n_heads = gw // HEAD_DIM
    tm = TOKEN_TILE
    kb = ATTN_BLOCK
    assert w_spatial.shape[1] == SPATIAL_CHUNK and s % tm == 0 and gw % ATTN_LANES == 0
    assert tm % (ATTN_SUBBLOCKS * kb) == 0 and tm % SPATIAL_CHUNK == 0
    row = lambda a: a.reshape(1, -1).astype(F32)

    q_gain = row(jnp.tile(q_norm_g, n_heads) * (HEAD_DIM ** -0.5 * math.log2(math.e)))
    k_gain = row(jnp.tile(k_norm_g, n_heads))
    head_of = jnp.arange(MXU_DIM) // HEAD_DIM
    gsum = (head_of[:, None] == head_of[None, :]).astype(BF16)
    pos = jnp.arange(kb)
    tri = jnp.kron(jnp.eye(2, dtype=BF16), -(pos[:, None] >= pos[None, :]).astype(BF16))
    bias = jnp.repeat(b_spatial.T.astype(F32), HEAD_DIM, axis=1)

    x2d = x.reshape(b * s, d)
    q, k, v, u, g = _in_proj(x2d, row(attn_norm_g), w_in.astype(BF16), q_gain, k_gain,
                             row(sgu_norm_g), gsum,
                             tm=IN_PROJ_TILE if (b * s) % IN_PROJ_TILE == 0 else tm)
    y = _attn_mix_ffn(q.reshape(b, s, gw), k.reshape(b, s, gw), v.reshape(b, s, gw), tri, x2d, u, g,
                      w_spatial, bias, row(sb_out_norm_g), row(sgu_out_norm_g), w_out.astype(BF16),
                      row(ffn_norm_g), w_gate.astype(BF16), w_up.astype(BF16), w_down.astype(BF16),
                      tm=tm)
    return y.reshape(b, s, d)
```

```python
import functools
import math

import jax
import jax.numpy as jnp
from jax import lax
from jax.experimental import pallas as pl
from jax.experimental.pallas import tpu as pltpu

EPS = 1e-6
HEAD_DIM = 64
SPATIAL_CHUNK = 128
STREAM_CHUNK = 64
LANES = 128
BF16_SUBLANES = 16
MXU_DIM = 256
TOKEN_TILE = 512
IN_PROJ_TILE = 1024
ATTN_LANES = MXU_DIM
ATTN_BLOCK = LANES
ATTN_SUBBLOCKS = 2
NEAR_BLOCKS = 2
FF_CHUNK = MXU_DIM
EARLY_SCORE_WINDOWS = 2
VMEM_LIMIT_BYTES = 60 * 1024 * 1024
LOG2_WEIGHT_UNDERFLOW = -150.0
NO_BLOCK_LOG2_WEIGHT = -1e30

F32 = jnp.float32
BF16 = jnp.bfloat16


def _rms_scale(v, width):
    return lax.rsqrt(jnp.sum(v * v, axis=-1, keepdims=True) * (1.0 / width) + EPS)


def _dot(a, b):
    return jnp.dot(a, b, preferred_element_type=F32)


def _in_proj_kernel(x_ref, ng_ref, w_ref, qg_ref, kg_ref, sg_ref, gsum_ref, *rest):
    n_slabs = (len(rest) - 6) // 2
    slabs_in, (q_ref, k_ref, v_ref, u_ref, g_ref) = rest[:n_slabs], rest[n_slabs:n_slabs + 5]
    slabs_out, wb_ref = rest[n_slabs + 5:-1], rest[-1]
    gw = q_ref.shape[-1]

    @pl.when(pl.program_id(0) == 0)
    def _():
        wb_ref[...] = w_ref[...].astype(BF16)

    for src, dst in zip(slabs_in, slabs_out):
        dst[...] = src[...].astype(BF16)

    x = x_ref[...]
    h = (x * _rms_scale(x, x.shape[-1]) * ng_ref[...]).astype(BF16)

    def head_norm(p, gain):
        p2 = (p * p).astype(BF16)
        ssq = jnp.concatenate([_dot(p2[:, l0:l0 + MXU_DIM], gsum_ref[...])
                               for l0 in range(0, gw, MXU_DIM)], axis=1)
        return p * lax.rsqrt(ssq * (1.0 / HEAD_DIM) + EPS) * gain

    gg = jax.nn.gelu(_dot(h, wb_ref[:, 4 * gw:5 * gw]))
    g_ref[...] = (gg * _rms_scale(gg, gw) * sg_ref[...]).astype(BF16)
    u_ref[...] = jax.nn.gelu(_dot(h, wb_ref[:, 3 * gw:4 * gw])).astype(BF16)
    q_ref[...] = head_norm(_dot(h, wb_ref[:, 0 * gw:1 * gw]), qg_ref[...]).astype(BF16)
    k_ref[...] = head_norm(_dot(h, wb_ref[:, 1 * gw:2 * gw]), kg_ref[...]).astype(BF16)
    v_ref[...] = _dot(h, wb_ref[:, 2 * gw:3 * gw]).astype(BF16)


def _slab_view(w, n_steps):
    for cols in sorted(set(w.shape), reverse=True):
        rows, rem = divmod(w.size, n_steps * cols)
        if rem == 0 and cols % LANES == 0 and rows % BF16_SUBLANES == 0:
            return w.reshape(n_steps, rows, cols)
    return None


def _in_proj(x2d, attn_norm_g, w_in, q_gain, k_gain, sgu_norm_g, gsum, later_weights, *, tm):
    n, d = x2d.shape
    gw = w_in.shape[1] // 5
    n_steps = n // tm
    row = lambda i: (i, 0)
    const = lambda i: (0, 0)
    slab = lambda i: (i, 0, 0)
    out = jax.ShapeDtypeStruct((n, gw), BF16)
    views = [_slab_view(w, n_steps) for w in later_weights]
    fused = [v for v in views if v is not None]
    outs = pl.pallas_call(
        _in_proj_kernel,
        grid=(n_steps,),
        in_specs=[
            pl.BlockSpec((tm, d), row),
            pl.BlockSpec((1, d), const),
            pl.BlockSpec((d, 5 * gw), const, pipeline_mode=pl.Buffered(1)),
            pl.BlockSpec((1, gw), const),
            pl.BlockSpec((1, gw), const),
            pl.BlockSpec((1, gw), const),
            pl.BlockSpec(gsum.shape, const),
        ] + [pl.BlockSpec((1,) + v.shape[1:], slab) for v in fused],
        out_specs=[pl.BlockSpec((tm, gw), row)] * 5
                  + [pl.BlockSpec((1,) + v.shape[1:], slab) for v in fused],
        out_shape=[out] * 5 + [jax.ShapeDtypeStruct(v.shape, BF16) for v in fused],
        scratch_shapes=[pltpu.VMEM(w_in.shape, BF16)],
        compiler_params=pltpu.CompilerParams(
            dimension_semantics=("arbitrary",), vmem_limit_bytes=VMEM_LIMIT_BYTES),
        name="in_proj",
    )(x2d, attn_norm_g, w_in, q_gain, k_gain, sgu_norm_g, gsum, *fused)
    cast = iter(outs[5:])
    weights_bf16 = [w.astype(BF16) if v is None else next(cast).reshape(w.shape)
                    for w, v in zip(later_weights, views)]
    return outs[:5], weights_bf16


def _attn_mix_ffn_kernel(q_ref, kt_ref, vt_ref, tri_ref, x_ref, u_ref, g_ref, wsp_ref, bias_ref,
                         sbg_ref, sgg_ref, wout_ref, fng_ref, wg_ref, wu_ref, wd_ref,
                         o_ref, k_ref, v_ref, osb_ref, acc_ref, osgu_ref, h_ref, *, tiles_per_seq):
    n = pl.program_id(0)
    tm, gw = u_ref.shape
    kb = ATTN_BLOCK
    n_groups_lanes = gw // ATTN_LANES
    n_qblocks = tm // kb
    n_heads = ATTN_LANES // HEAD_DIM
    m = n_heads * kb
    half = kb // 2
    slot = n % 2
    first_block = (jnp.minimum(n, pl.num_programs(0) - 2) % tiles_per_seq) * n_qblocks

    @pl.when(n == 0)
    def _():
        osb_ref[1] = jnp.zeros(osb_ref.shape[1:], osb_ref.dtype)

    tile_rows = pl.ds(pl.multiple_of(first_block * kb, tm), tm)
    k_ref[tile_rows, :] = kt_ref[0]
    v_ref[tile_rows, :] = vt_ref[0]

    lane_head = lax.broadcasted_iota(jnp.int32, (kb, ATTN_LANES), 1) // HEAD_DIM
    strict = (lax.broadcasted_iota(jnp.int32, (m, kb), 1)
              < lax.broadcasted_iota(jnp.int32, (m, kb), 0) % kb)
    tri = tri_ref[...]
    piece = lambda t, b: t[:, b * kb:(b + 1) * kb]

    def rows_of(ref, j, lanes):
        return ref[pl.ds(pl.multiple_of(j * kb, kb), kb), lanes]

    def softplus2(z):
        return jnp.maximum(z, 0.0) + jnp.log2(1.0 + jnp.exp2(-jnp.abs(z)))

    def qk(qs, kmat):
        return lax.dot_general(qs, kmat, (((1,), (1,)), ((), ())), preferred_element_type=F32)

    def stacked_q(c, lanes):
        q = q_ref[0, c * kb:(c + 1) * kb, lanes]
        return jnp.concatenate(
            [jnp.where(lane_head == p, q, jnp.zeros_like(q)) for p in range(n_heads)], axis=0)

    def half_of(t, h):
        return jnp.concatenate([t[p * kb + h * half:p * kb + (h + 1) * half] for p in range(n_heads)],
                               axis=0)

    def head_rows(pv, rows):
        head = lax.broadcasted_iota(jnp.int32, (rows, ATTN_LANES), 1) // HEAD_DIM
        out = pv[0:rows, :]
        for p in range(1, n_heads):
            out = jnp.where(head == p, pv[p * rows:(p + 1) * rows, :], out)
        return out

    carries = {}

    def window_stages(lanes, cs_):
        chains = range(len(cs_))
        near = [[first_block + c - b for b in range(NEAR_BLOCKS)] for c in cs_]
        far = [first_block + c - NEAR_BLOCKS for c in cs_]
        k_near = [jnp.concatenate([rows_of(k_ref, jnp.maximum(j, 0), lanes) for j in near[t]], axis=0)
                  for t in chains]
        qss = [stacked_q(c, lanes) for c in cs_]
        z = [qk(qss[t], k_near[t]) for t in chains]
        zf = [qk(half_of(qss[t], 0), rows_of(k_ref, jnp.maximum(far[t], 0), lanes))
              for t in chains]
        yield
        sp = [softplus2(z[t]) for t in chains]
        spf = [softplus2(zf[t]) for t in chains]
        near_pairs = [jnp.concatenate([jnp.where(strict, piece(sp[t], 0), 0.0).astype(BF16),
                                       piece(sp[t], 1).astype(BF16)], axis=1) for t in chains]
        far_pair = jnp.concatenate([spf[t].astype(BF16) for t in chains], axis=1)
        yield
        cs = _dot(jnp.concatenate(near_pairs + [far_pair], axis=0), tri)
        weights = []
        for t in chains:
            cs0, cs1 = (cs[t * m:(t + 1) * m, b * kb:(b + 1) * kb] for b in range(NEAR_BLOCKS))
            csf = cs[len(cs_) * m:, t * kb:(t + 1) * kb]
            a0 = jnp.where(strict, jnp.exp2(piece(z[t], 0) + cs0), 0.0)
            a1 = jnp.exp2((piece(z[t], 1) + cs1)
                          + jnp.where(near[t][1] >= 0, cs0[:, 0:1], NO_BLOCK_LOG2_WEIGHT))
            upper, lower = (half_of(cs0, h)[:, 0:1] + half_of(cs1, h)[:, 0:1] for h in range(2))
            af = jnp.exp2((zf[t] + csf) + jnp.where(far[t] >= 0, upper, NO_BLOCK_LOG2_WEIGHT))
            weights.append((jnp.concatenate([a0, a1], axis=1).astype(BF16), af.astype(BF16)))
            carries[(lanes.start, cs_[t], 0)] = upper + csf[:, 0:1]
            carries[(lanes.start, cs_[t], 1)] = lower
        yield
        for t in chains:
            v_near = jnp.concatenate([rows_of(v_ref, jnp.maximum(j, 0), lanes) for j in near[t]], axis=0)
            acc_ref[cs_[t], :, lanes] = head_rows(_dot(weights[t][0], v_near), kb)
            acc_ref[cs_[t], 0:half, lanes] += head_rows(
                _dot(weights[t][1], rows_of(v_ref, jnp.maximum(far[t], 0), lanes)), half)
        yield

    def remainder(lanes, c, h, carry):
        qs = half_of(stacked_q(c, lanes), h)

        def more(state):
            j, _, live = state
            return jnp.logical_and(j >= 0, live)

        def step(state):
            j, carry, _ = state
            z = qk(qs, rows_of(k_ref, j, lanes))
            cs = _dot(softplus2(z).astype(BF16), tri[0:kb, 0:kb])
            pv = _dot(jnp.exp2((z + cs) + carry).astype(BF16), rows_of(v_ref, j, lanes))
            acc_ref[c, h * half:(h + 1) * half, lanes] += head_rows(pv, half)
            carry = carry + cs[:, 0:1]
            return j - 1, carry, jnp.max(carry) > LOG2_WEIGHT_UNDERFLOW

        first_unvisited = first_block + c - NEAR_BLOCKS - (1 - h)
        lax.while_loop(more, step, (first_unvisited, carry, jnp.max(carry) > LOG2_WEIGHT_UNDERFLOW))

    def mix_ffn_stages():
        n_sgu = wsp_ref.shape[0]
        ri = lax.broadcasted_iota(jnp.int32, (SPATIAL_CHUNK, SPATIAL_CHUNK), 0) // STREAM_CHUNK
        ci = lax.broadcasted_iota(jnp.int32, (SPATIAL_CHUNK, SPATIAL_CHUNK), 1) // STREAM_CHUNK
        w = [jnp.where(ci <= ri, wsp_ref[gi], 0.0).astype(BF16) for gi in range(n_sgu)]
        low_half = lax.broadcasted_iota(jnp.int32, (SPATIAL_CHUNK, LANES), 1) < HEAD_DIM
        for p in range(gw // LANES):
            l0 = p * LANES
            wpair = jnp.concatenate([w[2 * p], w[2 * p + 1]], axis=1)
            for c in range(tm // SPATIAL_CHUNK):
                r0 = c * SPATIAL_CHUNK
                gp = g_ref[r0:r0 + SPATIAL_CHUNK, l0:l0 + LANES]
                zero = jnp.zeros_like(gp)
                gstack = jnp.concatenate(
                    [jnp.where(low_half, gp, zero), jnp.where(low_half, zero, gp)], axis=0)
                s = _dot(wpair, gstack) + bias_ref[:, l0:l0 + LANES]
                osgu_ref[r0:r0 + SPATIAL_CHUNK, l0:l0 + LANES] = (
                    u_ref[r0:r0 + SPATIAL_CHUNK, l0:l0 + LANES].astype(F32) * s)
        yield
        osgu = osgu_ref[...]
        sgn = (osgu * _rms_scale(osgu, gw) * sgg_ref[...]).astype(BF16)
        osb = osb_ref[1 - slot].astype(F32)
        sbn = (osb * _rms_scale(osb, gw) * sbg_ref[...]).astype(BF16)
        x1 = x_ref[...] + _dot(sbn, wout_ref[0:gw, :]) + _dot(sgn, wout_ref[gw:2 * gw, :])
        h_ref[...] = (x1 * _rms_scale(x1, x1.shape[-1]) * fng_ref[...]).astype(BF16)
        o_ref[...] = x1
        yield
        def gate_up(cols):
            h = h_ref[...]
            return _dot(h, wg_ref[:, cols]), _dot(h, wu_ref[:, cols])

        d_ff = wg_ref.shape[1]
        chunks = [slice(f0, min(f0 + FF_CHUNK, d_ff)) for f0 in range(0, d_ff, FF_CHUNK)]
        projected = gate_up(chunks[0])
        for i, cols in enumerate(chunks):
            gate, up = projected
            if i + 1 < len(chunks):
                projected = gate_up(chunks[i + 1])
            ff = (gate * jax.nn.sigmoid(gate) * up).astype(BF16)
            o_ref[...] += _dot(ff, wd_ref[cols, :])
            yield

    mixer = mix_ffn_stages()
    windows = [window_stages(slice(hg * ATTN_LANES, (hg + 1) * ATTN_LANES),
                             list(range(c0, c0 + ATTN_SUBBLOCKS)))
               for hg in range(n_groups_lanes) for c0 in range(0, n_qblocks, ATTN_SUBBLOCKS)]
    for win in windows[:EARLY_SCORE_WINDOWS]:
        next(win)
        next(mixer, None)
    for win in windows:
        for _ in win:
            next(mixer, None)
    for _ in mixer:
        pass

    slowest = functools.reduce(jnp.maximum, carries.values())
    @pl.when(jnp.max(slowest) > LOG2_WEIGHT_UNDERFLOW)
    def _():
        for (l0, c, h), carry in carries.items():
            remainder(slice(l0, l0 + ATTN_LANES), c, h, carry)

    for c in range(n_qblocks):
        osb_ref[slot, c * kb:(c + 1) * kb, :] = acc_ref[c].astype(BF16)


def _attn_mix_ffn(q, k, v, tri, x2d, u, g, w_spatial, bias, sb_gain, sgu_gain, w_out, ffn_gain,
                  w_gate, w_up, w_down, *, tm):
    b, s, gw = q.shape
    n_tok, d = x2d.shape
    tiles_per_seq = s // tm
    n_tiles = n_tok // tm
    assert w_gate.shape[1] % MXU_DIM == 0 and FF_CHUNK % MXU_DIM == 0
    attn_tile = lambda n: jnp.minimum(n, n_tiles - 1)
    mix_tile = lambda n: jnp.maximum(n - 1, 0)
    const = lambda n: (0, 0)
    resident = functools.partial(pl.BlockSpec, index_map=const, pipeline_mode=pl.Buffered(1))
    attn_block = pl.BlockSpec(
        (1, tm, gw), lambda n: (attn_tile(n) // tiles_per_seq, attn_tile(n) % tiles_per_seq, 0))
    return pl.pallas_call(
        functools.partial(_attn_mix_ffn_kernel, tiles_per_seq=tiles_per_seq),
        grid=(n_tiles + 1,),
        in_specs=[
            attn_block,
            attn_block,
            attn_block,
            resident(tri.shape),
            pl.BlockSpec((tm, d), lambda n: (mix_tile(n), 0)),
            pl.BlockSpec((tm, gw), lambda n: (mix_tile(n), 0)),
            pl.BlockSpec((tm, gw), lambda n: (mix_tile(n), 0)),
            pl.BlockSpec(w_spatial.shape, lambda n: (0, 0, 0), pipeline_mode=pl.Buffered(1)),
            resident(bias.shape),
            pl.BlockSpec((1, gw), const),
            pl.BlockSpec((1, gw), const),
            resident(w_out.shape),
            pl.BlockSpec((1, d), const),
            resident(w_gate.shape),
            resident(w_up.shape),
            resident(w_down.shape),
        ],
        out_specs=pl.BlockSpec((tm, d), lambda n: (mix_tile(n), 0)),
        out_shape=jax.ShapeDtypeStruct((n_tok, d), F32),
        scratch_shapes=[
            pltpu.VMEM((s, gw), BF16),
            pltpu.VMEM((s, gw), BF16),
            pltpu.VMEM((2, tm, gw), BF16),
            pltpu.VMEM((tm // ATTN_BLOCK, ATTN_BLOCK, gw), F32),
            pltpu.VMEM((tm, gw), F32),
            pltpu.VMEM((tm, d), BF16),
        ],
        compiler_params=pltpu.CompilerParams(
            dimension_semantics=("arbitrary",), vmem_limit_bytes=VMEM_LIMIT_BYTES),
        name="attn_mix_ffn",
    )(q, k, v, tri, x2d, u, g, w_spatial, bias, sb_gain, sgu_gain, w_out, ffn_gain,
      w_gate, w_up, w_down)


def kernel(x, attn_norm_g, w_in, q_norm_g, k_norm_g, sgu_norm_g, w_spatial, b_spatial,
           sb_out_norm_g, sgu_out_norm_g, w_out, ffn_norm_g, w_gate, w_up, w_down):
    b, s, d = x.shape
    gw = w_in.shape[1] // 5
    n_heads = gw // HEAD_DIM
    tm = TOKEN_TILE
    kb = ATTN_BLOCK
    assert w_spatial.shape[1] == SPATIAL_CHUNK and s % tm == 0 and gw % ATTN_LANES == 0
    assert tm % (ATTN_SUBBLOCKS * kb) == 0 and tm % SPATIAL_CHUNK == 0
    row = lambda a: a.reshape(1, -1).astype(F32)

    q_gain = row(jnp.tile(q_norm_g, n_heads) * (HEAD_DIM ** -0.5 * math.log2(math.e)))
    k_gain = row(jnp.tile(k_norm_g, n_heads))
    head_of = jnp.arange(MXU_DIM) // HEAD_DIM
    gsum = (head_of[:, None] == head_of[None, :]).astype(BF16)
    pos = jnp.arange(kb)
    tri = jnp.kron(jnp.eye(2, dtype=BF16), -(pos[:, None] >= pos[None, :]).astype(BF16))
    bias = jnp.repeat(b_spatial.T.astype(F32), HEAD_DIM, axis=1)

    x2d = x.reshape(b * s, d)
    (q, k, v, u, g), (w_out_b, w_gate_b, w_up_b, w_down_b) = _in_proj(
        x2d, row(attn_norm_g), w_in, q_gain, k_gain, row(sgu_norm_g), gsum,
        [w_out, w_gate, w_up, w_down], tm=IN_PROJ_TILE if (b * s) % IN_PROJ_TILE == 0 else tm)
    y = _attn_mix_ffn(q.reshape(b, s, gw), k.reshape(b, s, gw), v.reshape(b, s, gw), tri, x2d, u, g,
                      w_spatial, bias, row(sb_out_norm_g), row(sgu_out_norm_g), w_out_b,
                      row(ffn_norm_g), w_gate_b, w_up_b, w_down_b, tm=tm)
    return y.reshape(b, s, d)
```

```python
import functools
import math

import jax
import jax.numpy as jnp
from jax import lax
from jax.experimental import pallas as pl
from jax.experimental.pallas import tpu as pltpu

EPS = 1e-6
HEAD_DIM = 64
SPATIAL_CHUNK = 128
STREAM_CHUNK = 64
LANES = 128
F32_SUBLANES = 8
MXU_DIM = 256
TOKEN_TILE = 512
IN_PROJ_TILE = 1024
ATTN_LANES = MXU_DIM
ATTN_BLOCK = LANES
ATTN_SUBBLOCKS = 2
NEAR_BLOCKS = 2
FF_CHUNK = MXU_DIM
EARLY_SCORE_WINDOWS = 2
VMEM_LIMIT_BYTES = 60 * 1024 * 1024
LOG2_WEIGHT_UNDERFLOW = -150.0
NO_BLOCK_LOG2_WEIGHT = -1e30

F32 = jnp.float32
BF16 = jnp.bfloat16


def _rms_scale(v, width):
    return lax.rsqrt(jnp.sum(v * v, axis=-1, keepdims=True) * (1.0 / width) + EPS)


def _dot(a, b):
    return jnp.dot(a, b, preferred_element_type=F32)


def _in_proj_kernel(x_ref, ng_ref, w_ref, qg_ref, kg_ref, sg_ref, gsum_ref, *rest):
    n_slabs = (len(rest) - 6) // 2
    slabs_in, (q_ref, k_ref, v_ref, u_ref, g_ref) = rest[:n_slabs], rest[n_slabs:n_slabs + 5]
    slabs_out, wb_ref = rest[n_slabs + 5:-1], rest[-1]
    gw = q_ref.shape[-1]

    @pl.when(pl.program_id(0) == 0)
    def _():
        wb_ref[...] = w_ref[...].astype(BF16)

    for src, dst in zip(slabs_in, slabs_out):
        dst[...] = src[...].astype(BF16)

    x = x_ref[...]
    h = (x * _rms_scale(x, x.shape[-1]) * ng_ref[...]).astype(BF16)

    def head_norm(p, gain):
        p2 = (p * p).astype(BF16)
        ssq = jnp.concatenate([_dot(p2[:, l0:l0 + MXU_DIM], gsum_ref[...])
                               for l0 in range(0, gw, MXU_DIM)], axis=1)
        return p * lax.rsqrt(ssq * (1.0 / HEAD_DIM) + EPS) * gain

    gg = jax.nn.gelu(_dot(h, wb_ref[:, 4 * gw:5 * gw]))
    g_ref[...] = (gg * _rms_scale(gg, gw) * sg_ref[...]).astype(BF16)
    u_ref[...] = jax.nn.gelu(_dot(h, wb_ref[:, 3 * gw:4 * gw])).astype(BF16)
    q_ref[...] = head_norm(_dot(h, wb_ref[:, 0 * gw:1 * gw]), qg_ref[...]).astype(BF16)
    k_ref[...] = head_norm(_dot(h, wb_ref[:, 1 * gw:2 * gw]), kg_ref[...]).astype(BF16)
    v_ref[...] = _dot(h, wb_ref[:, 2 * gw:3 * gw]).astype(BF16)


def _slab_view(w, n_steps):
    rows, rem = divmod(w.shape[0], n_steps)
    if rem == 0 and rows % F32_SUBLANES == 0 and w.shape[1] % LANES == 0:
        return w.reshape(n_steps, rows, w.shape[1])
    return None


def _in_proj(x2d, attn_norm_g, w_in, q_gain, k_gain, sgu_norm_g, gsum, later_weights, *, tm):
    n, d = x2d.shape
    gw = w_in.shape[1] // 5
    n_steps = n // tm
    row = lambda i: (i, 0)
    const = lambda i: (0, 0)
    slab = lambda i: (i, 0, 0)
    out = jax.ShapeDtypeStruct((n, gw), BF16)
    views = [_slab_view(w, n_steps) for w in later_weights]
    fused = [v for v in views if v is not None]
    outs = pl.pallas_call(
        _in_proj_kernel,
        grid=(n_steps,),
        in_specs=[
            pl.BlockSpec((tm, d), row),
            pl.BlockSpec((1, d), const),
            pl.BlockSpec((d, 5 * gw), const, pipeline_mode=pl.Buffered(1)),
            pl.BlockSpec((1, gw), const),
            pl.BlockSpec((1, gw), const),
            pl.BlockSpec((1, gw), const),
            pl.BlockSpec(gsum.shape, const),
        ] + [pl.BlockSpec((1,) + v.shape[1:], slab) for v in fused],
        out_specs=[pl.BlockSpec((tm, gw), row)] * 5
                  + [pl.BlockSpec((1,) + v.shape[1:], slab) for v in fused],
        out_shape=[out] * 5 + [jax.ShapeDtypeStruct(v.shape, BF16) for v in fused],
        scratch_shapes=[pltpu.VMEM(w_in.shape, BF16)],
        compiler_params=pltpu.CompilerParams(
            dimension_semantics=("arbitrary",), vmem_limit_bytes=VMEM_LIMIT_BYTES),
        name="in_proj",
    )(x2d, attn_norm_g, w_in, q_gain, k_gain, sgu_norm_g, gsum, *fused)
    cast = iter(outs[5:])
    weights_bf16 = [w.astype(BF16) if v is None else next(cast).reshape(w.shape)
                    for w, v in zip(later_weights, views)]
    return outs[:5], weights_bf16


def _attn_mix_ffn_kernel(q_ref, kt_ref, vt_ref, tri_ref, x_ref, u_ref, g_ref, wsp_ref, bias_ref,
                         sbg_ref, sgg_ref, wout_ref, fng_ref, wg_ref, wu_ref, wd_ref,
                         o_ref, k_ref, v_ref, osb_ref, acc_ref, osgu_ref, h_ref, *, tiles_per_seq):
    n = pl.program_id(0)
    tm, gw = u_ref.shape
    kb = ATTN_BLOCK
    n_groups_lanes = gw // ATTN_LANES
    n_qblocks = tm // kb
    n_heads = ATTN_LANES // HEAD_DIM
    m = n_heads * kb
    half = kb // 2
    slot = n % 2
    first_block = (jnp.minimum(n, pl.num_programs(0) - 2) % tiles_per_seq) * n_qblocks

    @pl.when(n == 0)
    def _():
        osb_ref[1] = jnp.zeros(osb_ref.shape[1:], osb_ref.dtype)

    tile_rows = pl.ds(pl.multiple_of(first_block * kb, tm), tm)
    k_ref[tile_rows, :] = kt_ref[0]
    v_ref[tile_rows, :] = vt_ref[0]

    lane_head = lax.broadcasted_iota(jnp.int32, (kb, ATTN_LANES), 1) // HEAD_DIM
    strict = (lax.broadcasted_iota(jnp.int32, (m, kb), 1)
              < lax.broadcasted_iota(jnp.int32, (m, kb), 0) % kb)
    tri = tri_ref[...]
    piece = lambda t, b: t[:, b * kb:(b + 1) * kb]

    def rows_of(ref, j, lanes):
        return ref[pl.ds(pl.multiple_of(j * kb, kb), kb), lanes]

    def softplus2(z):
        return jnp.maximum(z, 0.0) + jnp.log2(1.0 + jnp.exp2(-jnp.abs(z)))

    def qk(qs, kmat):
        return lax.dot_general(qs, kmat, (((1,), (1,)), ((), ())), preferred_element_type=F32)

    def stacked_q(c, lanes):
        q = q_ref[0, c * kb:(c + 1) * kb, lanes]
        return jnp.concatenate(
            [jnp.where(lane_head == p, q, jnp.zeros_like(q)) for p in range(n_heads)], axis=0)

    def half_of(t, h):
        return jnp.concatenate([t[p * kb + h * half:p * kb + (h + 1) * half] for p in range(n_heads)],
                               axis=0)

    def head_rows(pv, rows):
        head = lax.broadcasted_iota(jnp.int32, (rows, ATTN_LANES), 1) // HEAD_DIM
        out = pv[0:rows, :]
        for p in range(1, n_heads):
            out = jnp.where(head == p, pv[p * rows:(p + 1) * rows, :], out)
        return out

    carries = {}

    def window_stages(lanes, cs_):
        chains = range(len(cs_))
        near = [[first_block + c - b for b in range(NEAR_BLOCKS)] for c in cs_]
        far = [first_block + c - NEAR_BLOCKS for c in cs_]
        k_near = [jnp.concatenate([rows_of(k_ref, jnp.maximum(j, 0), lanes) for j in near[t]], axis=0)
                  for t in chains]
        qss = [stacked_q(c, lanes) for c in cs_]
        z = [qk(qss[t], k_near[t]) for t in chains]
        zf = [qk(half_of(qss[t], 0), rows_of(k_ref, jnp.maximum(far[t], 0), lanes))
              for t in chains]
        yield
        sp = [softplus2(z[t]) for t in chains]
        spf = [softplus2(zf[t]) for t in chains]
        near_pairs = [jnp.concatenate([jnp.where(strict, piece(sp[t], 0), 0.0).astype(BF16),
                                       piece(sp[t], 1).astype(BF16)], axis=1) for t in chains]
        far_pair = jnp.concatenate([spf[t].astype(BF16) for t in chains], axis=1)
        yield
        cs = _dot(jnp.concatenate(near_pairs + [far_pair], axis=0), tri)
        weights = []
        for t in chains:
            cs0, cs1 = (cs[t * m:(t + 1) * m, b * kb:(b + 1) * kb] for b in range(NEAR_BLOCKS))
            csf = cs[len(cs_) * m:, t * kb:(t + 1) * kb]
            a0 = jnp.where(strict, jnp.exp2(piece(z[t], 0) + cs0), 0.0)
            a1 = jnp.exp2((piece(z[t], 1) + cs1)
                          + jnp.where(near[t][1] >= 0, cs0[:, 0:1], NO_BLOCK_LOG2_WEIGHT))
            upper, lower = (half_of(cs0, h)[:, 0:1] + half_of(cs1, h)[:, 0:1] for h in range(2))
            af = jnp.exp2((zf[t] + csf) + jnp.where(far[t] >= 0, upper, NO_BLOCK_LOG2_WEIGHT))
            weights.append((jnp.concatenate([a0, a1], axis=1).astype(BF16), af.astype(BF16)))
            carries[(lanes.start, cs_[t], 0)] = upper + csf[:, 0:1]
            carries[(lanes.start, cs_[t], 1)] = lower
        yield
        for t in chains:
            v_near = jnp.concatenate([rows_of(v_ref, jnp.maximum(j, 0), lanes) for j in near[t]], axis=0)
            acc_ref[cs_[t], :, lanes] = head_rows(_dot(weights[t][0], v_near), kb)
            acc_ref[cs_[t], 0:half, lanes] += head_rows(
                _dot(weights[t][1], rows_of(v_ref, jnp.maximum(far[t], 0), lanes)), half)
        yield

    def remainder(lanes, c, h, carry):
        qs = half_of(stacked_q(c, lanes), h)

        def more(state):
            j, _, live = state
            return jnp.logical_and(j >= 0, live)

        def step(state):
            j, carry, _ = state
            z = qk(qs, rows_of(k_ref, j, lanes))
            cs = _dot(softplus2(z).astype(BF16), tri[0:kb, 0:kb])
            pv = _dot(jnp.exp2((z + cs) + carry).astype(BF16), rows_of(v_ref, j, lanes))
            acc_ref[c, h * half:(h + 1) * half, lanes] += head_rows(pv, half)
            carry = carry + cs[:, 0:1]
            return j - 1, carry, jnp.max(carry) > LOG2_WEIGHT_UNDERFLOW

        first_unvisited = first_block + c - NEAR_BLOCKS - (1 - h)
        lax.while_loop(more, step, (first_unvisited, carry, jnp.max(carry) > LOG2_WEIGHT_UNDERFLOW))

    def mix_ffn_stages():
        n_sgu = wsp_ref.shape[0]
        ri = lax.broadcasted_iota(jnp.int32, (SPATIAL_CHUNK, SPATIAL_CHUNK), 0) // STREAM_CHUNK
        ci = lax.broadcasted_iota(jnp.int32, (SPATIAL_CHUNK, SPATIAL_CHUNK), 1) // STREAM_CHUNK
        w = [jnp.where(ci <= ri, wsp_ref[gi], 0.0).astype(BF16) for gi in range(n_sgu)]
        low_half = lax.broadcasted_iota(jnp.int32, (SPATIAL_CHUNK, LANES), 1) < HEAD_DIM
        for p in range(gw // LANES):
            l0 = p * LANES
            wpair = jnp.concatenate([w[2 * p], w[2 * p + 1]], axis=1)
            for c in range(tm // SPATIAL_CHUNK):
                r0 = c * SPATIAL_CHUNK
                gp = g_ref[r0:r0 + SPATIAL_CHUNK, l0:l0 + LANES]
                zero = jnp.zeros_like(gp)
                gstack = jnp.concatenate(
                    [jnp.where(low_half, gp, zero), jnp.where(low_half, zero, gp)], axis=0)
                s = _dot(wpair, gstack) + bias_ref[:, l0:l0 + LANES]
                osgu_ref[r0:r0 + SPATIAL_CHUNK, l0:l0 + LANES] = (
                    u_ref[r0:r0 + SPATIAL_CHUNK, l0:l0 + LANES].astype(F32) * s)
        yield
        osgu = osgu_ref[...]
        sgn = (osgu * _rms_scale(osgu, gw) * sgg_ref[...]).astype(BF16)
        osb = osb_ref[1 - slot].astype(F32)
        sbn = (osb * _rms_scale(osb, gw) * sbg_ref[...]).astype(BF16)
        x1 = x_ref[...] + _dot(sbn, wout_ref[0:gw, :]) + _dot(sgn, wout_ref[gw:2 * gw, :])
        h_ref[...] = (x1 * _rms_scale(x1, x1.shape[-1]) * fng_ref[...]).astype(BF16)
        o_ref[...] = x1
        yield
        def gate_up(cols):
            h = h_ref[...]
            return _dot(h, wg_ref[:, cols]), _dot(h, wu_ref[:, cols])

        d_ff = wg_ref.shape[1]
        chunks = [slice(f0, min(f0 + FF_CHUNK, d_ff)) for f0 in range(0, d_ff, FF_CHUNK)]
        projected = gate_up(chunks[0])
        for i, cols in enumerate(chunks):
            gate, up = projected
            if i + 1 < len(chunks):
                projected = gate_up(chunks[i + 1])
            ff = (gate * jax.nn.sigmoid(gate) * up).astype(BF16)
            o_ref[...] += _dot(ff, wd_ref[cols, :])
            yield

    mixer = mix_ffn_stages()
    windows = [window_stages(slice(hg * ATTN_LANES, (hg + 1) * ATTN_LANES),
                             list(range(c0, c0 + ATTN_SUBBLOCKS)))
               for hg in range(n_groups_lanes) for c0 in range(0, n_qblocks, ATTN_SUBBLOCKS)]
    for win in windows[:EARLY_SCORE_WINDOWS]:
        next(win)
        next(mixer, None)
    for win in windows:
        for _ in win:
            next(mixer, None)
    for _ in mixer:
        pass

    slowest = functools.reduce(jnp.maximum, carries.values())
    @pl.when(jnp.max(slowest) > LOG2_WEIGHT_UNDERFLOW)
    def _():
        for (l0, c, h), carry in carries.items():
            remainder(slice(l0, l0 + ATTN_LANES), c, h, carry)

    for c in range(n_qblocks):
        osb_ref[slot, c * kb:(c + 1) * kb, :] = acc_ref[c].astype(BF16)


def _attn_mix_ffn(q, k, v, tri, x2d, u, g, w_spatial, bias, sb_gain, sgu_gain, w_out, ffn_gain,
                  w_gate, w_up, w_down, *, tm):
    b, s, gw = q.shape
    n_tok, d = x2d.shape
    tiles_per_seq = s // tm
    n_tiles = n_tok // tm
    assert w_gate.shape[1] % MXU_DIM == 0 and FF_CHUNK % MXU_DIM == 0
    attn_tile = lambda n: jnp.minimum(n, n_tiles - 1)
    mix_tile = lambda n: jnp.maximum(n - 1, 0)
    const = lambda n: (0, 0)
    resident = functools.partial(pl.BlockSpec, index_map=const, pipeline_mode=pl.Buffered(1))
    attn_block = pl.BlockSpec(
        (1, tm, gw), lambda n: (attn_tile(n) // tiles_per_seq, attn_tile(n) % tiles_per_seq, 0))
    return pl.pallas_call(
        functools.partial(_attn_mix_ffn_kernel, tiles_per_seq=tiles_per_seq),
        grid=(n_tiles + 1,),
        in_specs=[
            attn_block,
            attn_block,
            attn_block,
            resident(tri.shape),
            pl.BlockSpec((tm, d), lambda n: (mix_tile(n), 0)),
            pl.BlockSpec((tm, gw), lambda n: (mix_tile(n), 0)),
            pl.BlockSpec((tm, gw), lambda n: (mix_tile(n), 0)),
            pl.BlockSpec(w_spatial.shape, lambda n: (0, 0, 0), pipeline_mode=pl.Buffered(1)),
            resident(bias.shape),
            pl.BlockSpec((1, gw), const),
            pl.BlockSpec((1, gw), const),
            resident(w_out.shape),
            pl.BlockSpec((1, d), const),
            resident(w_gate.shape),
            resident(w_up.shape),
            resident(w_down.shape),
        ],
        out_specs=pl.BlockSpec((tm, d), lambda n: (mix_tile(n), 0)),
        out_shape=jax.ShapeDtypeStruct((n_tok, d), F32),
        scratch_shapes=[
            pltpu.VMEM((s, gw), BF16),
            pltpu.VMEM((s, gw), BF16),
            pltpu.VMEM((2, tm, gw), BF16),
            pltpu.VMEM((tm // ATTN_BLOCK, ATTN_BLOCK, gw), F32),
            pltpu.VMEM((tm, gw), F32),
            pltpu.VMEM((tm, d), BF16),
        ],
        compiler_params=pltpu.CompilerParams(
            dimension_semantics=("arbitrary",), vmem_limit_bytes=VMEM_LIMIT_BYTES),
        name="attn_mix_ffn",
    )(q, k, v, tri, x2d, u, g, w_spatial, bias, sb_gain, sgu_gain, w_out, ffn_gain,
      w_gate, w_up, w_down)


def kernel(x, attn_norm_g, w_in, q_norm_g, k_norm_g, sgu_norm_g, w_spatial, b_spatial,
           sb_out_norm_g, sgu_out_norm_g, w_out, ffn_norm_g, w_gate, w_up, w_down):
    b, s, d = x.shape
    gw = w_in.shape[1] // 5
    n_heads = gw // HEAD_DIM
    tm = TOKEN_TILE
    kb = ATTN_BLOCK
    assert w_spatial.shape[1] == SPATIAL_CHUNK and s % tm == 0 and gw % ATTN_LANES == 0
    assert tm % (ATTN_SUBBLOCKS * kb) == 0 and tm % SPATIAL_CHUNK == 0
    row = lambda a: a.reshape(1, -1).astype(F32)

    q_gain = row(jnp.tile(q_norm_g, n_heads) * (HEAD_DIM ** -0.5 * math.log2(math.e)))
    k_gain = row(jnp.tile(k_norm_g, n_heads))
    head_of = jnp.arange(MXU_DIM) // HEAD_DIM
    gsum = (head_of[:, None] == head_of[None, :]).astype(BF16)
    pos = jnp.arange(kb)
    tri = jnp.kron(jnp.eye(2, dtype=BF16), -(pos[:, None] >= pos[None, :]).astype(BF16))
    bias = jnp.repeat(b_spatial.T.astype(F32), HEAD_DIM, axis=1)

    x2d = x.reshape(b * s, d)
    (q, k, v, u, g), (w_out_b, w_gate_b, w_up_b, w_down_b) = _in_proj(
        x2d, row(attn_norm_g), w_in, q_gain, k_gain, row(sgu_norm_g), gsum,
        [w_out, w_gate, w_up, w_down], tm=IN_PROJ_TILE if (b * s) % IN_PROJ_TILE == 0 else tm)
    y = _attn_mix_ffn(q.reshape(b, s, gw), k.reshape(b, s, gw), v.reshape(b, s, gw), tri, x2d, u, g,
                      w_spatial, bias, row(sb_out_norm_g), row(sgu_out_norm_g), w_out_b,
                      row(ffn_norm_g), w_gate_b, w_up_b, w_down_b, tm=tm)
    return y.reshape(b, s, d)
```

```python
import functools
import math

import jax
import jax.numpy as jnp
from jax import lax
from jax.experimental import pallas as pl
from jax.experimental.pallas import tpu as pltpu

EPS = 1e-6
HEAD_DIM = 64
SPATIAL_CHUNK = 128
STREAM_CHUNK = 64
LANES = 128
F32_SUBLANES = 8
MXU_DIM = 256
TOKEN_TILE = 512
IN_PROJ_TILE = 1024
ATTN_LANES = MXU_DIM
ATTN_BLOCK = LANES
ATTN_SUBBLOCKS = 2
NEAR_BLOCKS = 2
FAR_ROWS = 32
FF_CHUNK = MXU_DIM
EARLY_SCORE_WINDOWS = 2
VMEM_LIMIT_BYTES = 60 * 1024 * 1024
LOG2_WEIGHT_UNDERFLOW = -150.0
NO_BLOCK_LOG2_WEIGHT = -1e30

F32 = jnp.float32
BF16 = jnp.bfloat16


def _rms_scale(v, width):
    return lax.rsqrt(jnp.sum(v * v, axis=-1, keepdims=True) * (1.0 / width) + EPS)


def _dot(a, b):
    return jnp.dot(a, b, preferred_element_type=F32)


def _in_proj_kernel(x_ref, ng_ref, w_ref, qg_ref, kg_ref, sg_ref, gsum_ref, *rest):
    n_slabs = (len(rest) - 6) // 2
    slabs_in, (q_ref, k_ref, v_ref, u_ref, g_ref) = rest[:n_slabs], rest[n_slabs:n_slabs + 5]
    slabs_out, wb_ref = rest[n_slabs + 5:-1], rest[-1]
    gw = q_ref.shape[-1]

    @pl.when(pl.program_id(0) == 0)
    def _():
        wb_ref[...] = w_ref[...].astype(BF16)

    for src, dst in zip(slabs_in, slabs_out):
        dst[...] = src[...].astype(BF16)

    x = x_ref[...]
    h = (x * _rms_scale(x, x.shape[-1]) * ng_ref[...]).astype(BF16)

    def head_norm(p, gain):
        p2 = (p * p).astype(BF16)
        ssq = jnp.concatenate([_dot(p2[:, l0:l0 + MXU_DIM], gsum_ref[...])
                               for l0 in range(0, gw, MXU_DIM)], axis=1)
        return p * lax.rsqrt(ssq * (1.0 / HEAD_DIM) + EPS) * gain

    gg = jax.nn.gelu(_dot(h, wb_ref[:, 4 * gw:5 * gw]))
    g_ref[...] = (gg * _rms_scale(gg, gw) * sg_ref[...]).astype(BF16)
    u_ref[...] = jax.nn.gelu(_dot(h, wb_ref[:, 3 * gw:4 * gw])).astype(BF16)
    q_ref[...] = head_norm(_dot(h, wb_ref[:, 0 * gw:1 * gw]), qg_ref[...]).astype(BF16)
    k_ref[...] = head_norm(_dot(h, wb_ref[:, 1 * gw:2 * gw]), kg_ref[...]).astype(BF16)
    v_ref[...] = _dot(h, wb_ref[:, 2 * gw:3 * gw]).astype(BF16)


def _slab_view(w, n_steps):
    rows, rem = divmod(w.shape[0], n_steps)
    if rem == 0 and rows % F32_SUBLANES == 0 and w.shape[1] % LANES == 0:
        return w.reshape(n_steps, rows, w.shape[1])
    return None


def _in_proj(x2d, attn_norm_g, w_in, q_gain, k_gain, sgu_norm_g, gsum, later_weights, *, tm):
    n, d = x2d.shape
    gw = w_in.shape[1] // 5
    n_steps = n // tm
    row = lambda i: (i, 0)
    const = lambda i: (0, 0)
    slab = lambda i: (i, 0, 0)
    out = jax.ShapeDtypeStruct((n, gw), BF16)
    views = [_slab_view(w, n_steps) for w in later_weights]
    fused = [v for v in views if v is not None]
    outs = pl.pallas_call(
        _in_proj_kernel,
        grid=(n_steps,),
        in_specs=[
            pl.BlockSpec((tm, d), row),
            pl.BlockSpec((1, d), const),
            pl.BlockSpec((d, 5 * gw), const, pipeline_mode=pl.Buffered(1)),
            pl.BlockSpec((1, gw), const),
            pl.BlockSpec((1, gw), const),
            pl.BlockSpec((1, gw), const),
            pl.BlockSpec(gsum.shape, const),
        ] + [pl.BlockSpec((1,) + v.shape[1:], slab) for v in fused],
        out_specs=[pl.BlockSpec((tm, gw), row)] * 5
                  + [pl.BlockSpec((1,) + v.shape[1:], slab) for v in fused],
        out_shape=[out] * 5 + [jax.ShapeDtypeStruct(v.shape, BF16) for v in fused],
        scratch_shapes=[pltpu.VMEM(w_in.shape, BF16)],
        compiler_params=pltpu.CompilerParams(
            dimension_semantics=("arbitrary",), vmem_limit_bytes=VMEM_LIMIT_BYTES),
        name="in_proj",
    )(x2d, attn_norm_g, w_in, q_gain, k_gain, sgu_norm_g, gsum, *fused)
    cast = iter(outs[5:])
    weights_bf16 = [w.astype(BF16) if v is None else next(cast).reshape(w.shape)
                    for w, v in zip(later_weights, views)]
    return outs[:5], weights_bf16


def _attn_mix_ffn_kernel(q_ref, kt_ref, vt_ref, tri_ref, x_ref, u_ref, g_ref, wsp_ref, bias_ref,
                         sbg_ref, sgg_ref, wout_ref, fng_ref, wg_ref, wu_ref, wd_ref,
                         o_ref, k_ref, v_ref, osb_ref, acc_ref, osgu_ref, h_ref, *, tiles_per_seq):
    n = pl.program_id(0)
    tm, gw = u_ref.shape
    kb = ATTN_BLOCK
    n_groups_lanes = gw // ATTN_LANES
    n_qblocks = tm // kb
    n_heads = ATTN_LANES // HEAD_DIM
    m = n_heads * kb
    part_rows = (FAR_ROWS, kb - FAR_ROWS)
    part_start = (0, FAR_ROWS)
    slot = n % 2
    first_block = (jnp.minimum(n, pl.num_programs(0) - 2) % tiles_per_seq) * n_qblocks

    @pl.when(n == 0)
    def _():
        osb_ref[1] = jnp.zeros(osb_ref.shape[1:], osb_ref.dtype)

    tile_rows = pl.ds(pl.multiple_of(first_block * kb, tm), tm)
    k_ref[tile_rows, :] = kt_ref[0]
    v_ref[tile_rows, :] = vt_ref[0]

    lane_head = lax.broadcasted_iota(jnp.int32, (kb, ATTN_LANES), 1) // HEAD_DIM
    strict = (lax.broadcasted_iota(jnp.int32, (m, kb), 1)
              < lax.broadcasted_iota(jnp.int32, (m, kb), 0) % kb)
    tri = tri_ref[...]
    piece = lambda t, b: t[:, b * kb:(b + 1) * kb]

    def rows_of(ref, j, lanes):
        return ref[pl.ds(pl.multiple_of(j * kb, kb), kb), lanes]

    def softplus2(z):
        return jnp.maximum(z, 0.0) + jnp.log2(1.0 + jnp.exp2(-jnp.abs(z)))

    def qk(qs, kmat):
        return lax.dot_general(qs, kmat, (((1,), (1,)), ((), ())), preferred_element_type=F32)

    def stacked_q(c, lanes):
        q = q_ref[0, c * kb:(c + 1) * kb, lanes]
        return jnp.concatenate(
            [jnp.where(lane_head == p, q, jnp.zeros_like(q)) for p in range(n_heads)], axis=0)

    def part_of(t, h):
        r0, rows = part_start[h], part_rows[h]
        return jnp.concatenate([t[p * kb + r0:p * kb + r0 + rows] for p in range(n_heads)], axis=0)

    def head_rows(pv, rows):
        head = lax.broadcasted_iota(jnp.int32, (rows, ATTN_LANES), 1) // HEAD_DIM
        out = pv[0:rows, :]
        for p in range(1, n_heads):
            out = jnp.where(head == p, pv[p * rows:(p + 1) * rows, :], out)
        return out

    carries = {}

    def window_stages(lanes, cs_):
        chains = range(len(cs_))
        near = [[first_block + c - b for b in range(NEAR_BLOCKS)] for c in cs_]
        far = [first_block + c - NEAR_BLOCKS for c in cs_]
        k_near = [jnp.concatenate([rows_of(k_ref, jnp.maximum(j, 0), lanes) for j in near[t]], axis=0)
                  for t in chains]
        qss = [stacked_q(c, lanes) for c in cs_]
        z = [qk(qss[t], k_near[t]) for t in chains]
        zf = [qk(part_of(qss[t], 0), rows_of(k_ref, jnp.maximum(far[t], 0), lanes))
              for t in chains]
        yield
        sp = [softplus2(z[t]) for t in chains]
        spf = [softplus2(zf[t]) for t in chains]
        near_pairs = [jnp.concatenate([jnp.where(strict, piece(sp[t], 0), 0.0).astype(BF16),
                                       piece(sp[t], 1).astype(BF16)], axis=1) for t in chains]
        far_pair = jnp.concatenate([spf[t].astype(BF16) for t in chains], axis=1)
        yield
        cs = _dot(jnp.concatenate(near_pairs + [far_pair], axis=0), tri)
        weights = []
        for t in chains:
            cs0, cs1 = (cs[t * m:(t + 1) * m, b * kb:(b + 1) * kb] for b in range(NEAR_BLOCKS))
            csf = cs[len(cs_) * m:, t * kb:(t + 1) * kb]
            a0 = jnp.where(strict, jnp.exp2(piece(z[t], 0) + cs0), 0.0)
            a1 = jnp.exp2((piece(z[t], 1) + cs1)
                          + jnp.where(near[t][1] >= 0, cs0[:, 0:1], NO_BLOCK_LOG2_WEIGHT))
            upper, lower = (part_of(cs0, h)[:, 0:1] + part_of(cs1, h)[:, 0:1] for h in range(2))
            af = jnp.exp2((zf[t] + csf) + jnp.where(far[t] >= 0, upper, NO_BLOCK_LOG2_WEIGHT))
            weights.append((jnp.concatenate([a0, a1], axis=1).astype(BF16), af.astype(BF16)))
            carries[(lanes.start, cs_[t], 0)] = upper + csf[:, 0:1]
            carries[(lanes.start, cs_[t], 1)] = lower
        yield
        for t in chains:
            v_near = jnp.concatenate([rows_of(v_ref, jnp.maximum(j, 0), lanes) for j in near[t]], axis=0)
            acc_ref[cs_[t], :, lanes] = head_rows(_dot(weights[t][0], v_near), kb)
            acc_ref[cs_[t], 0:FAR_ROWS, lanes] += head_rows(
                _dot(weights[t][1], rows_of(v_ref, jnp.maximum(far[t], 0), lanes)), FAR_ROWS)
        yield

    def remainder(lanes, c, h, carry):
        qs = part_of(stacked_q(c, lanes), h)
        r0, rows = part_start[h], part_rows[h]

        def more(state):
            j, _, live = state
            return jnp.logical_and(j >= 0, live)

        def step(state):
            j, carry, _ = state
            z = qk(qs, rows_of(k_ref, j, lanes))
            cs = _dot(softplus2(z).astype(BF16), tri[0:kb, 0:kb])
            pv = _dot(jnp.exp2((z + cs) + carry).astype(BF16), rows_of(v_ref, j, lanes))
            acc_ref[c, r0:r0 + rows, lanes] += head_rows(pv, rows)
            carry = carry + cs[:, 0:1]
            return j - 1, carry, jnp.max(carry) > LOG2_WEIGHT_UNDERFLOW

        first_unvisited = first_block + c - NEAR_BLOCKS - (1 - h)
        lax.while_loop(more, step, (first_unvisited, carry, jnp.max(carry) > LOG2_WEIGHT_UNDERFLOW))

    def mix_ffn_stages():
        n_sgu = wsp_ref.shape[0]
        ri = lax.broadcasted_iota(jnp.int32, (SPATIAL_CHUNK, SPATIAL_CHUNK), 0) // STREAM_CHUNK
        ci = lax.broadcasted_iota(jnp.int32, (SPATIAL_CHUNK, SPATIAL_CHUNK), 1) // STREAM_CHUNK
        w = [jnp.where(ci <= ri, wsp_ref[gi], 0.0).astype(BF16) for gi in range(n_sgu)]
        low_half = lax.broadcasted_iota(jnp.int32, (SPATIAL_CHUNK, LANES), 1) < HEAD_DIM
        for p in range(gw // LANES):
            l0 = p * LANES
            wpair = jnp.concatenate([w[2 * p], w[2 * p + 1]], axis=1)
            for c in range(tm // SPATIAL_CHUNK):
                r0 = c * SPATIAL_CHUNK
                gp = g_ref[r0:r0 + SPATIAL_CHUNK, l0:l0 + LANES]
                zero = jnp.zeros_like(gp)
                gstack = jnp.concatenate(
                    [jnp.where(low_half, gp, zero), jnp.where(low_half, zero, gp)], axis=0)
                s = _dot(wpair, gstack) + bias_ref[:, l0:l0 + LANES]
                osgu_ref[r0:r0 + SPATIAL_CHUNK, l0:l0 + LANES] = (
                    u_ref[r0:r0 + SPATIAL_CHUNK, l0:l0 + LANES].astype(F32) * s)
        yield
        osgu = osgu_ref[...]
        sgn = (osgu * _rms_scale(osgu, gw) * sgg_ref[...]).astype(BF16)
        osb = osb_ref[1 - slot].astype(F32)
        sbn = (osb * _rms_scale(osb, gw) * sbg_ref[...]).astype(BF16)
        x1 = x_ref[...] + _dot(sbn, wout_ref[0:gw, :]) + _dot(sgn, wout_ref[gw:2 * gw, :])
        h_ref[...] = (x1 * _rms_scale(x1, x1.shape[-1]) * fng_ref[...]).astype(BF16)
        o_ref[...] = x1
        yield
        def gate_up(cols):
            h = h_ref[...]
            return _dot(h, wg_ref[:, cols]), _dot(h, wu_ref[:, cols])

        d_ff = wg_ref.shape[1]
        chunks = [slice(f0, min(f0 + FF_CHUNK, d_ff)) for f0 in range(0, d_ff, FF_CHUNK)]
        projected = gate_up(chunks[0])
        for i, cols in enumerate(chunks):
            gate, up = projected
            if i + 1 < len(chunks):
                projected = gate_up(chunks[i + 1])
            ff = (gate * jax.nn.sigmoid(gate) * up).astype(BF16)
            o_ref[...] += _dot(ff, wd_ref[cols, :])
            yield

    mixer = mix_ffn_stages()
    windows = [window_stages(slice(hg * ATTN_LANES, (hg + 1) * ATTN_LANES),
                             list(range(c0, c0 + ATTN_SUBBLOCKS)))
               for hg in range(n_groups_lanes) for c0 in range(0, n_qblocks, ATTN_SUBBLOCKS)]
    for win in windows[:EARLY_SCORE_WINDOWS]:
        next(win)
        next(mixer, None)
    for win in windows:
        for _ in win:
            next(mixer, None)
    for _ in mixer:
        pass

    slowest = functools.reduce(jnp.maximum, [jnp.max(c, axis=0, keepdims=True) for c in carries.values()])
    @pl.when(jnp.max(slowest) > LOG2_WEIGHT_UNDERFLOW)
    def _():
        for (l0, c, h), carry in carries.items():
            remainder(slice(l0, l0 + ATTN_LANES), c, h, carry)

    for c in range(n_qblocks):
        osb_ref[slot, c * kb:(c + 1) * kb, :] = acc_ref[c].astype(BF16)


def _attn_mix_ffn(q, k, v, tri, x2d, u, g, w_spatial, bias, sb_gain, sgu_gain, w_out, ffn_gain,
                  w_gate, w_up, w_down, *, tm):
    b, s, gw = q.shape
    n_tok, d = x2d.shape
    tiles_per_seq = s // tm
    n_tiles = n_tok // tm
    assert w_gate.shape[1] % MXU_DIM == 0 and FF_CHUNK % MXU_DIM == 0
    attn_tile = lambda n: jnp.minimum(n, n_tiles - 1)
    mix_tile = lambda n: jnp.maximum(n - 1, 0)
    const = lambda n: (0, 0)
    resident = functools.partial(pl.BlockSpec, index_map=const, pipeline_mode=pl.Buffered(1))
    attn_block = pl.BlockSpec(
        (1, tm, gw), lambda n: (attn_tile(n) // tiles_per_seq, attn_tile(n) % tiles_per_seq, 0))
    return pl.pallas_call(
        functools.partial(_attn_mix_ffn_kernel, tiles_per_seq=tiles_per_seq),
        grid=(n_tiles + 1,),
        in_specs=[
            attn_block,
            attn_block,
            attn_block,
            resident(tri.shape),
            pl.BlockSpec((tm, d), lambda n: (mix_tile(n), 0)),
            pl.BlockSpec((tm, gw), lambda n: (mix_tile(n), 0)),
            pl.BlockSpec((tm, gw), lambda n: (mix_tile(n), 0)),
            pl.BlockSpec(w_spatial.shape, lambda n: (0, 0, 0), pipeline_mode=pl.Buffered(1)),
            resident(bias.shape),
            pl.BlockSpec((1, gw), const),
            pl.BlockSpec((1, gw), const),
            resident(w_out.shape),
            pl.BlockSpec((1, d), const),
            resident(w_gate.shape),
            resident(w_up.shape),
            resident(w_down.shape),
        ],
        out_specs=pl.BlockSpec((tm, d), lambda n: (mix_tile(n), 0)),
        out_shape=jax.ShapeDtypeStruct((n_tok, d), F32),
        scratch_shapes=[
            pltpu.VMEM((s, gw), BF16),
            pltpu.VMEM((s, gw), BF16),
            pltpu.VMEM((2, tm, gw), BF16),
            pltpu.VMEM((tm // ATTN_BLOCK, ATTN_BLOCK, gw), F32),
            pltpu.VMEM((tm, gw), F32),
            pltpu.VMEM((tm, d), BF16),
        ],
        compiler_params=pltpu.CompilerParams(
            dimension_semantics=("arbitrary",), vmem_limit_bytes=VMEM_LIMIT_BYTES),
        name="attn_mix_ffn",
    )(q, k, v, tri, x2d, u, g, w_spatial, bias, sb_gain, sgu_gain, w_out, ffn_gain,
      w_gate, w_up, w_down)


def kernel(x, attn_norm_g, w_in, q_norm_g, k_norm_g, sgu_norm_g, w_spatial, b_spatial,
           sb_out_norm_g, sgu_out_norm_g, w_out, ffn_norm_g, w_gate, w_up, w_down):
    b, s, d = x.shape
    gw = w_in.shape[1] // 5
    n_heads = gw // HEAD_DIM
    tm = TOKEN_TILE
    kb = ATTN_BLOCK
    assert w_spatial.shape[1] == SPATIAL_CHUNK and s % tm == 0 and gw % ATTN_LANES == 0
    assert tm % (ATTN_SUBBLOCKS * kb) == 0 and tm % SPATIAL_CHUNK == 0
    row = lambda a: a.reshape(1, -1).astype(F32)

    q_gain = row(jnp.tile(q_norm_g, n_heads) * (HEAD_DIM ** -0.5 * math.log2(math.e)))
    k_gain = row(jnp.tile(k_norm_g, n_heads))
    head_of = jnp.arange(MXU_DIM) // HEAD_DIM
    gsum = (head_of[:, None] == head_of[None, :]).astype(BF16)
    pos = jnp.arange(kb)
    tri = jnp.kron(jnp.eye(2, dtype=BF16), -(pos[:, None] >= pos[None, :]).astype(BF16))
    bias = jnp.repeat(b_spatial.T.astype(F32), HEAD_DIM, axis=1)

    x2d = x.reshape(b * s, d)
    (q, k, v, u, g), (w_out_b, w_gate_b, w_up_b, w_down_b) = _in_proj(
        x2d, row(attn_norm_g), w_in, q_gain, k_gain, row(sgu_norm_g), gsum,
        [w_out, w_gate, w_up, w_down], tm=IN_PROJ_TILE if (b * s) % IN_PROJ_TILE == 0 else tm)
    y = _attn_mix_ffn(q.reshape(b, s, gw), k.reshape(b, s, gw), v.reshape(b, s, gw), tri, x2d, u, g,
                      w_spatial, bias, row(sb_out_norm_g), row(sgu_out_norm_g), w_out_b,
                      row(ffn_norm_g), w_gate_b, w_up_b, w_down_b, tm=tm)
    return y.reshape(b, s, d)
```
